```python
import jax, jax.numpy as jnp
from jax import lax
import numpy as np

D_MODEL = 1024
BATCH = 8
SEQ = 4096
DEPTH = 1

HEAD_DIM = 64
SB_HEADS = 8
RWKV_HEADS = 8
SB_WIDTH = SB_HEADS * HEAD_DIM
RWKV_WIDTH = RWKV_HEADS * HEAD_DIM
MIX_WIDTH = SB_WIDTH + RWKV_WIDTH
DECAY_LORA = 64
AAA_LORA = 64
GATE_LORA = 128
D_FF = 2816
Q_BLOCK = 128
NORM_EPS = 1e-6
GN_EPS = HEAD_DIM * 1e-5
SHIFT_WIDTH = 3 * RWKV_WIDTH + DECAY_LORA + AAA_LORA + GATE_LORA
IN_WIDTH = 3 * SB_WIDTH + SHIFT_WIDTH

kernel_name = "hybrid_stickbreaking_rwkv7_macaron_layer"


def rmsnorm(x, g):
    xf = x.astype(jnp.float32)
    y = xf * lax.rsqrt(jnp.mean(xf * xf, axis=-1, keepdims=True) + NORM_EPS)
    return (y * g.astype(jnp.float32)).astype(x.dtype)


def swiglu(x, w_gate, w_up, w_down):
    return (jax.nn.silu(x @ w_gate) * (x @ w_up)) @ w_down


def stick_breaking_attention(q, k, v):
    S = q.shape[2]
    d = q.shape[3]
    scale = d ** -0.5
    outs = []
    for start in range(0, S, Q_BLOCK):
        end = start + Q_BLOCK
        qb = q[:, :, start:end]
        kb = k[:, :, :end]
        vb = v[:, :, :end]
        z = jnp.einsum('bhqd,bhkd->bhqk', qb, kb).astype(jnp.float32) * scale
        t_idx = start + jnp.arange(Q_BLOCK)[:, None]
        s_idx = jnp.arange(end)[None, :]
        causal = s_idx < t_idx
        log_keep = jnp.where(causal, jax.nn.log_sigmoid(-z), 0.0)
        later = lax.cumsum(log_keep, axis=3, reverse=True) - log_keep
        weights = jnp.where(causal, jnp.exp(jax.nn.log_sigmoid(z) + later), 0.0)
        outs.append(jnp.einsum('bhqk,bhkd->bhqd', weights.astype(v.dtype), vb))
    return jnp.concatenate(outs, axis=2)


def rwkv7_scan(r, w, k, v, a_vec, b_vec):
    B, _, H, N = r.shape

    def step(state, inp):
        r_t, w_t, k_t, v_t, a_t, b_t = inp
        sa = jnp.einsum('bhij,bhj->bhi', state, a_t)
        state = (state * w_t[:, :, None, :]
                 + sa[..., None] * b_t[:, :, None, :]
                 + v_t[..., None] * k_t[:, :, None, :])
        y = jnp.einsum('bhij,bhj->bhi', state, r_t)
        return state, y

    xs = tuple(jnp.moveaxis(t.astype(jnp.float32), 1, 0) for t in (r, w, k, v, a_vec, b_vec))
    state0 = jnp.zeros((B, H, N, N), jnp.float32)
    _, ys = lax.scan(step, state0, xs)
    return jnp.moveaxis(ys, 0, 1)


def hybrid_mixer(h, w_in, shift_mu, sb_out_g, decay_w0, decay_w2, iclr_a0, iclr_a2,
                 gate_w2, k_k, k_a, r_k, gn_g, gn_b, w_out):
    B, S, _ = h.shape
    p = h @ w_in
    p_sb = p[..., :3 * SB_WIDTH]
    p_rw = p[..., 3 * SB_WIDTH:]

    q, k, v = jnp.split(p_sb, 3, axis=-1)
    to_heads = lambda t: t.reshape(B, S, SB_HEADS, HEAD_DIM).transpose(0, 2, 1, 3)
    sb = stick_breaking_attention(to_heads(q), to_heads(k), to_heads(v))
    sb = sb.transpose(0, 2, 1, 3)
    sb = rmsnorm(sb, sb_out_g.reshape(SB_HEADS, HEAD_DIM)).reshape(B, S, SB_WIDTH)

    p_prev = jnp.pad(p_rw, ((0, 0), (1, 0), (0, 0)))[:, :-1]
    p_rw = p_rw + (p_prev - p_rw) * shift_mu
    splits = [RWKV_WIDTH, 2 * RWKV_WIDTH, 3 * RWKV_WIDTH,
              3 * RWKV_WIDTH + DECAY_LORA, 3 * RWKV_WIDTH + DECAY_LORA + AAA_LORA]
    r, kr, vr, wd, ad, gd = jnp.split(p_rw, splits, axis=-1)
    w_log = -jax.nn.softplus(-(decay_w0 + jnp.tanh(wd) @ decay_w2)) - 0.5
    decay = jnp.exp(-jnp.exp(w_log.astype(jnp.float32)))
    a = jax.nn.sigmoid(iclr_a0 + ad @ iclr_a2)
    g = jax.nn.sigmoid(gd) @ gate_w2
    heads = lambda t: t.reshape(B, S, RWKV_HEADS, HEAD_DIM)
    kk = heads((kr * k_k).astype(jnp.float32))
    kk = kk / jnp.maximum(jnp.linalg.norm(kk, axis=-1, keepdims=True), 1e-12)
    kr = kr * (1.0 + (a - 1.0) * k_a)
    r_h, k_h, v_h, a_h = heads(r), heads(kr), heads(vr), heads(a)
    y = rwkv7_scan(r_h, heads(decay), k_h, v_h, -kk, kk * a_h.astype(jnp.float32))
    mu = jnp.mean(y, axis=-1, keepdims=True)
    var = jnp.mean(jnp.square(y - mu), axis=-1, keepdims=True)
    y = (y - mu) * lax.rsqrt(var + GN_EPS)
    y = y * gn_g.reshape(RWKV_HEADS, HEAD_DIM).astype(jnp.float32) + gn_b.reshape(RWKV_HEADS, HEAD_DIM).astype(jnp.float32)
    bonus = jnp.sum((r_h * k_h * r_k).astype(jnp.float32), axis=-1, keepdims=True) * v_h.astype(jnp.float32)
    y = (y + bonus).astype(h.dtype).reshape(B, S, RWKV_WIDTH) * g

    return jnp.concatenate([sb, y], axis=-1) @ w_out


def setup_inputs(seed: int = 0) -> dict:
    key = jax.random.key(seed)
    ks = jax.random.split(key, 32)
    L = DEPTH
    nrm = lambda k, shape, s: jax.random.normal(k, shape, jnp.float32) * s
    gain = lambda k, n: 1.0 + nrm(k, (L, n), 0.02)
    return {
        "x": jax.random.normal(ks[0], (BATCH, SEQ, D_MODEL), jnp.float32),
        "ffn1_pre_g": gain(ks[1], D_MODEL),
        "ffn1_post_g": gain(ks[2], D_MODEL),
        "ffn1_w_gate": nrm(ks[3], (L, D_MODEL, D_FF), D_MODEL ** -0.5),
        "ffn1_w_up": nrm(ks[4], (L, D_MODEL, D_FF), D_MODEL ** -0.5),
        "ffn1_w_down": nrm(ks[5], (L, D_FF, D_MODEL), D_FF ** -0.5),
        "mix_pre_g": gain(ks[6], D_MODEL),
        "mix_post_g": gain(ks[7], D_MODEL),
        "w_in": nrm(ks[8], (L, D_MODEL, IN_WIDTH), D_MODEL ** -0.5),
        "shift_mu": jax.random.uniform(ks[9], (L, SHIFT_WIDTH), jnp.float32),
        "sb_out_g": gain(ks[10], SB_WIDTH),
        "decay_w0": jax.random.uniform(ks[11], (L, RWKV_WIDTH), jnp.float32, -6.0, 1.0),
        "decay_w2": nrm(ks[12], (L, DECAY_LORA, RWKV_WIDTH), 0.1 * DECAY_LORA ** -0.5),
        "iclr_a0": nrm(ks[13], (L, RWKV_WIDTH), 0.1),
        "iclr_a2": nrm(ks[14], (L, AAA_LORA, RWKV_WIDTH), 0.1 * AAA_LORA ** -0.5),
        "gate_w2": nrm(ks[15], (L, GATE_LORA, RWKV_WIDTH), GATE_LORA ** -0.5),
        "k_k": 0.85 + nrm(ks[16], (L, RWKV_WIDTH), 0.05),
        "k_a": 1.0 + nrm(ks[17], (L, RWKV_WIDTH), 0.05),
        "r_k": nrm(ks[18], (L, RWKV_HEADS, HEAD_DIM), 0.1),
        "gn_g": gain(ks[19], RWKV_WIDTH),
        "gn_b": nrm(ks[20], (L, RWKV_WIDTH), 0.02),
        "w_out": nrm(ks[21], (L, MIX_WIDTH, D_MODEL), MIX_WIDTH ** -0.5),
        "ffn2_pre_g": gain(ks[22], D_MODEL),
        "ffn2_post_g": gain(ks[23], D_MODEL),
        "ffn2_w_gate": nrm(ks[24], (L, D_MODEL, D_FF), D_MODEL ** -0.5),
        "ffn2_w_up": nrm(ks[25], (L, D_MODEL, D_FF), D_MODEL ** -0.5),
        "ffn2_w_down": nrm(ks[26], (L, D_FF, D_MODEL), D_FF ** -0.5),
    }


def reference(x, ffn1_pre_g, ffn1_post_g, ffn1_w_gate, ffn1_w_up, ffn1_w_down,
              mix_pre_g, mix_post_g, w_in, shift_mu, sb_out_g, decay_w0, decay_w2,
              iclr_a0, iclr_a2, gate_w2, k_k, k_a, r_k, gn_g, gn_b, w_out,
              ffn2_pre_g, ffn2_post_g, ffn2_w_gate, ffn2_w_up, ffn2_w_down):
    for l in range(DEPTH):
        f = swiglu(rmsnorm(x, ffn1_pre_g[l]), ffn1_w_gate[l], ffn1_w_up[l], ffn1_w_down[l])
        x = x + 0.5 * rmsnorm(f, ffn1_post_g[l])
        m = hybrid_mixer(rmsnorm(x, mix_pre_g[l]), w_in[l], shift_mu[l], sb_out_g[l],
                         decay_w0[l], decay_w2[l], iclr_a0[l], iclr_a2[l], gate_w2[l],
                         k_k[l], k_a[l], r_k[l], gn_g[l], gn_b[l], w_out[l])
        x = x + rmsnorm(m, mix_post_g[l])
        f = swiglu(rmsnorm(x, ffn2_pre_g[l]), ffn2_w_gate[l], ffn2_w_up[l], ffn2_w_down[l])
        x = x + 0.5 * rmsnorm(f, ffn2_post_g[l])
    return x
```

```python
import functools

import jax
import jax.numpy as jnp
from jax import lax
from jax.experimental import pallas as pl
from jax.experimental.pallas import tpu as pltpu

F32 = jnp.float32
BF16 = jnp.bfloat16

HEAD_DIM = 64
SB_HEADS = 8
RWKV_HEADS = 8
SB_WIDTH = SB_HEADS * HEAD_DIM
RWKV_WIDTH = RWKV_HEADS * HEAD_DIM
DECAY_LORA = 64
AAA_LORA = 64
GATE_LORA = 128
NORM_EPS = 1e-6
GN_EPS = HEAD_DIM * 1e-5

LANES = 128
PAIR = LANES
CHUNK = 64
VMEM_LIMIT = 56 * 1024 * 1024

_NT = (((1,), (1,)), ((), ()))
_TN = (((0,), (0,)), ((), ()))


def _dot(a, b, dims=None, precision=None):
    if dims is None:
        dims = (((a.ndim - 1,), (0,)), ((), ()))
    return lax.dot_general(a, b, dims, precision=precision, preferred_element_type=F32)


def _dot_f32(a, b):
    return _dot(a, b, precision=lax.Precision.HIGHEST)


def _rms(x, g):
    return x * lax.rsqrt(jnp.mean(x * x, axis=-1, keepdims=True) + NORM_EPS) * g


def _softplus(x):
    return jnp.maximum(x, 0.0) + jnp.log1p(jnp.exp(-jnp.abs(x)))


def _sigmoid(x):
    return 1.0 / (1.0 + jnp.exp(-x))


def _head_ones(n):
    r = lax.broadcasted_iota(jnp.int32, (n, n), 0) // HEAD_DIM
    c = lax.broadcasted_iota(jnp.int32, (n, n), 1) // HEAD_DIM
    return (r == c).astype(F32)


def _ffn_body(x_ref, pre_ref, post_ref, wg_ref, wu_ref, wd_ref, o_ref):
    x = x_ref[...]
    xn = _rms(x, pre_ref[...]).astype(BF16)
    g = _dot(xn, wg_ref[...])
    u = _dot(xn, wu_ref[...])
    h = (g * _sigmoid(g) * u).astype(BF16)
    f = _dot(h, wd_ref[...])
    o_ref[...] = x + 0.5 * _rms(f, post_ref[...])


def _ffn(x2d, pre_g, post_g, wg, wu, wd, tm):
    t, d = x2d.shape
    dff = wg.shape[1]
    const = lambda i: (0, 0)
    return pl.pallas_call(
        _ffn_body,
        grid=(t // tm,),
        in_specs=[
            pl.BlockSpec((tm, d), lambda i: (i, 0)),
            pl.BlockSpec((1, d), const),
            pl.BlockSpec((1, d), const),
            pl.BlockSpec((d, dff), const, pipeline_mode=pl.Buffered(1)),
            pl.BlockSpec((d, dff), const, pipeline_mode=pl.Buffered(1)),
            pl.BlockSpec((dff, d), const, pipeline_mode=pl.Buffered(1)),
        ],
        out_specs=pl.BlockSpec((tm, d), lambda i: (i, 0)),
        out_shape=jax.ShapeDtypeStruct((t, d), F32),
        compiler_params=pltpu.CompilerParams(
            dimension_semantics=("arbitrary",), vmem_limit_bytes=VMEM_LIMIT),
        name="ffn",
    )(x2d, pre_g, post_g, wg, wu, wd)


def _inproj_body(x_ref, g_ref, w_ref, qkv_ref, prw_ref):
    xn = _rms(x_ref[...], g_ref[...]).astype(BF16)
    p = _dot(xn, w_ref[...])
    scale = HEAD_DIM ** -0.5
    qkv_ref[:, :SB_WIDTH] = (p[:, :SB_WIDTH] * scale).astype(BF16)
    qkv_ref[:, SB_WIDTH:] = p[:, SB_WIDTH:3 * SB_WIDTH].astype(BF16)
    prw_ref[...] = p[:, 3 * SB_WIDTH:]


def _inproj(x2d, g, w_in, tm):
    t, d = x2d.shape
    n = w_in.shape[1]
    nsb = 3 * SB_WIDTH
    const = lambda i: (0, 0)
    return pl.pallas_call(
        _inproj_body,
        grid=(t // tm,),
        in_specs=[
            pl.BlockSpec((tm, d), lambda i: (i, 0)),
            pl.BlockSpec((1, d), const),
            pl.BlockSpec((d, n), const, pipeline_mode=pl.Buffered(1)),
        ],
        out_specs=[
            pl.BlockSpec((tm, nsb), lambda i: (i, 0)),
            pl.BlockSpec((tm, n - nsb), lambda i: (i, 0)),
        ],
        out_shape=[
            jax.ShapeDtypeStruct((t, nsb), BF16),
            jax.ShapeDtypeStruct((t, n - nsb), F32),
        ],
        compiler_params=pltpu.CompilerParams(
            dimension_semantics=("arbitrary",), vmem_limit_bytes=VMEM_LIMIT),
        name="inproj",
    )(x2d, g, w_in)


def _sb_body(q_ref, k_ref, v_ref, g_ref, o_ref, acc_ref, *, blk):
    qi = pl.program_id(2)
    q = q_ref[0]
    lane = lax.broadcasted_iota(jnp.int32, (blk, PAIR), 1)
    first = lane < HEAD_DIM
    zero = jnp.zeros_like(q)
    q_heads = (jnp.where(first, q, zero), jnp.where(first, zero, q))
    row = lax.broadcasted_iota(jnp.int32, (blk, blk), 0)
    col = lax.broadcasted_iota(jnp.int32, (blk, blk), 1)
    suffix = (row >= col).astype(BF16)
    causal = col < row

    def visit(kb, carries, diagonal):
        start = pl.multiple_of(kb * blk, blk)
        kblk = k_ref[0, pl.ds(start, blk), :]
        vblk = v_ref[0, pl.ds(start, blk), :]
        new = []
        for h in range(2):
            z = _dot(q_heads[h], kblk, _NT)
            log_keep = -_softplus(z)
            if diagonal:
                log_keep = jnp.where(causal, log_keep, 0.0)
            c = _dot(log_keep.astype(BF16), suffix) + carries[h]
            w = jnp.exp(z + c)
            if diagonal:
                w = jnp.where(causal, w, 0.0)
            acc_ref[h] += _dot(w.astype(BF16), vblk)
            new.append(c[:, 0:1])
        return tuple(new)

    acc_ref[...] = jnp.zeros_like(acc_ref)
    zc = jnp.zeros((blk, 1), F32)
    carries = visit(qi, (zc, zc), True)
    lax.fori_loop(0, qi, lambda i, cs: visit(qi - 1 - i, cs, False), carries)

    out = jnp.where(first, acc_ref[0], acc_ref[1])
    ms = _dot_f32(out * out, _head_ones(PAIR)) * (1.0 / HEAD_DIM)
    o_ref[0] = (out * lax.rsqrt(ms + NORM_EPS) * g_ref[...]).astype(BF16)


def _sb_attention(qkv, sb_g, blk):
    b, s, _ = qkv.shape
    npair = SB_WIDTH // PAIR
    return pl.pallas_call(
        functools.partial(_sb_body, blk=blk),
        grid=(b, npair, s // blk),
        in_specs=[
            pl.BlockSpec((1, blk, PAIR), lambda bi, p, qi: (bi, qi, p)),
            pl.BlockSpec((1, s, PAIR), lambda bi, p, qi: (bi, 0, npair + p)),
            pl.BlockSpec((1, s, PAIR), lambda bi, p, qi: (bi, 0, 2 * npair + p)),
            pl.BlockSpec((1, PAIR), lambda bi, p, qi: (0, p)),
        ],
        out_specs=pl.BlockSpec((1, blk, PAIR), lambda bi, p, qi: (bi, qi, p)),
        out_shape=jax.ShapeDtypeStruct((b, s, SB_WIDTH), BF16),
        scratch_shapes=[pltpu.VMEM((2, blk, PAIR), F32)],
        compiler_params=pltpu.CompilerParams(
            dimension_semantics=("arbitrary", "arbitrary", "arbitrary"),
            vmem_limit_bytes=VMEM_LIMIT),
        name="sb_attention",
    )(qkv, qkv, qkv, sb_g)


def _rwkv_body(p_ref, mu_ref, w0_ref, dw2_ref, a0_ref, aw2_ref, gw2_ref, kk_ref, ka_ref,
               rk_ref, gng_ref, gnb_ref, o_ref,
               last_ref, state_ref, rt_ref, at_ref, bt_ref, kt_ref, v_ref, bh_ref, kh_ref,
               wc_ref, y_ref, *, lb):
    j = pl.program_id(1)
    w = RWKV_WIDTH

    @pl.when(j == 0)
    def _():
        last_ref[...] = jnp.zeros_like(last_ref)
        state_ref[...] = jnp.zeros_like(state_ref)

    p = p_ref[0]
    rows = lax.broadcasted_iota(jnp.int32, p.shape, 0)
    prev = jnp.where(rows == 0, last_ref[...], pltpu.roll(p, 1, 0))
    last_ref[...] = p[lb - 1:lb, :]
    xs = p + (prev - p) * mu_ref[...]
    r = xs[:, 0:w]
    kr = xs[:, w:2 * w]
    vr = xs[:, 2 * w:3 * w]
    lora_in = xs[:, 3 * w:3 * w + DECAY_LORA + AAA_LORA]
    gd = xs[:, 3 * w + DECAY_LORA + AAA_LORA:]
    lora_w = _dot(jnp.tanh(lora_in).astype(BF16), dw2_ref[...])
    lora_a = _dot(lora_in.astype(BF16), aw2_ref[...])
    gate = _dot(_sigmoid(gd).astype(BF16), gw2_ref[...])
    w_log = -_softplus(-(w0_ref[...] + lora_w)) - 0.5
    lw = -jnp.exp(w_log)
    a = _sigmoid(a0_ref[...] + lora_a)
    kk = kr * kk_ref[...]
    k2 = kr * (1.0 + (a - 1.0) * ka_ref[...])
    rkr = r * k2 * rk_ref[...]
    ones_h = _head_ones(PAIR)
    ssq_parts, bonus_parts = [], []
    for pi in range(w // PAIR):
        sl = slice(pi * PAIR, (pi + 1) * PAIR)
        ssq_parts.append(_dot_f32(kk[:, sl] * kk[:, sl], ones_h))
        bonus_parts.append(_dot_f32(rkr[:, sl], ones_h))
    ssq = jnp.concatenate(ssq_parts, axis=1)
    bonus = jnp.concatenate(bonus_parts, axis=1) * vr
    kkn = kk * lax.rsqrt(jnp.maximum(ssq, 1e-24))
    a_vec = -kkn
    b_vec = kkn * a

    tr = lax.broadcasted_iota(jnp.int32, (lb, lb), 0)
    tc = lax.broadcasted_iota(jnp.int32, (lb, lb), 1)
    same_chunk = (tr // CHUNK) == (tc // CHUNK)
    cum = _dot_f32((same_chunk & (tc <= tr)).astype(F32), lw)
    tot = _dot_f32(same_chunk.astype(F32), lw)
    dec_in = jnp.exp(cum)
    dec_out = jnp.exp(-cum)
    dec_rest = jnp.exp(tot - cum)
    rt_ref[...] = r * dec_in
    at_ref[...] = a_vec * jnp.exp(cum - lw)
    bt_ref[...] = b_vec * dec_out
    kt_ref[...] = k2 * dec_out
    v_ref[...] = vr
    bh_ref[...] = b_vec * dec_rest
    kh_ref[...] = k2 * dec_rest
    wc_ref[...] = jnp.exp(tot)

    n2 = 2 * CHUNK
    lane = lax.broadcasted_iota(jnp.int32, (CHUNK, PAIR), 1)
    first = lane < HEAD_DIM
    br = lax.broadcasted_iota(jnp.int32, (n2, n2), 0)
    bc = lax.broadcasted_iota(jnp.int32, (n2, n2), 1)
    same_head = (br // CHUNK) == (bc // CHUNK)
    strict = same_head & (bc < br)
    incl = same_head & (bc <= br)
    eye = br == bc

    def stack_masked(x):
        return jnp.concatenate([jnp.where(first, x, 0.0), jnp.where(first, 0.0, x)], axis=0)

    def stack(x):
        return jnp.concatenate([x, x], axis=0)

    def chunk_step(c, carry):
        t0 = pl.multiple_of(c * CHUNK, CHUNK)
        for pi in range(w // PAIR):
            sl = (pl.ds(t0, CHUNK), slice(pi * PAIR, (pi + 1) * PAIR))
            a_sm = stack_masked(at_ref[sl])
            r_sm = stack_masked(rt_ref[sl])
            v_sm = stack_masked(v_ref[sl]).astype(BF16)
            bh_sm = stack_masked(bh_ref[sl]).astype(BF16)
            kh_sm = stack_masked(kh_ref[sl]).astype(BF16)
            lhs = jnp.concatenate([a_sm, r_sm], axis=0).astype(BF16)
            rhs = jnp.concatenate([stack(bt_ref[sl]), stack(kt_ref[sl])], axis=0).astype(BF16)
            mm = _dot(lhs, rhs, _NT)
            m_ab = jnp.where(strict, mm[:n2, :n2], 0.0)
            m_ak = jnp.where(strict, mm[:n2, n2:], 0.0)
            m_rb = jnp.where(incl, mm[n2:, :n2], 0.0)
            m_rk = jnp.where(incl, mm[n2:, n2:], 0.0)
            x = jnp.concatenate([a_sm, _dot(m_ak.astype(BF16), v_sm)], axis=1)
            lp = m_ab
            steps = CHUNK.bit_length() - 1
            for i in range(steps):
                lp16 = lp.astype(BF16)
                if i < steps - 1:
                    both = _dot(lp16, jnp.concatenate([lp16, x.astype(BF16)], axis=1))
                    lp = both[:, :n2]
                    x = x + both[:, n2:]
                else:
                    x = x + _dot(lp16, x.astype(BF16))
            pq = x.astype(BF16)
            lhs2 = jnp.concatenate([m_rb, m_rk], axis=1).astype(BF16)
            low = jnp.concatenate([jnp.zeros((n2, PAIR), BF16), v_sm], axis=1)
            gy = _dot(lhs2, jnp.concatenate([pq, low], axis=0))
            g_sm = r_sm + gy[:, :PAIR]
            g_pk = g_sm[:CHUNK] + g_sm[CHUNK:]
            y0_pk = gy[:CHUNK, PAIR:] + gy[CHUNK:, PAIR:]
            wc = wc_ref[pl.ds(t0, 8), pi * PAIR:(pi + 1) * PAIR][0:1, :]
            phi = _dot(pq[:, :PAIR], bh_sm, _TN) + jnp.where(eye, wc, 0.0)
            psi = _dot(jnp.concatenate([pq[:, PAIR:], v_sm], axis=0),
                       jnp.concatenate([bh_sm, kh_sm], axis=0), _TN)
            s0 = state_ref[pi]
            s016 = s0.astype(BF16)
            y_ref[sl] = _dot(g_pk.astype(BF16), s016, _NT) + y0_pk
            state_ref[pi] = _dot(s016, phi.astype(BF16)) + psi
        return carry

    lax.fori_loop(0, lb // CHUNK, chunk_step, 0)

    y = y_ref[...]
    mean_parts, var_parts = [], []
    inv = 1.0 / HEAD_DIM
    for pi in range(w // PAIR):
        sl = slice(pi * PAIR, (pi + 1) * PAIR)
        m = _dot_f32(y[:, sl], ones_h) * inv
        d = y[:, sl] - m
        mean_parts.append(d)
        var_parts.append(_dot_f32(d * d, ones_h) * inv)
    d = jnp.concatenate(mean_parts, axis=1)
    var = jnp.concatenate(var_parts, axis=1)
    yn = d * lax.rsqrt(var + GN_EPS) * gng_ref[...] + gnb_ref[...]
    o_ref[0] = ((yn + bonus) * gate).astype(BF16)


def _rwkv(prw, mu, w0, dw2, a0, aw2, gw2, k_k, k_a, r_k, gn_g, gn_b, lb):
    b, s, n = prw.shape
    w = RWKV_WIDTH
    const = lambda bi, j: (0, 0)
    vec = lambda m: pl.BlockSpec((1, m), const)
    big = lambda: pltpu.VMEM((lb, w), F32)
    return pl.pallas_call(
        functools.partial(_rwkv_body, lb=lb),
        grid=(b, s // lb),
        in_specs=[
            pl.BlockSpec((1, lb, n), lambda bi, j: (bi, j, 0)),
            vec(n), vec(w),
            pl.BlockSpec((DECAY_LORA + AAA_LORA, w), const),
            vec(w),
            pl.BlockSpec((DECAY_LORA + AAA_LORA, w), const),
            pl.BlockSpec((GATE_LORA, w), const),
            vec(w), vec(w), vec(w), vec(w), vec(w),
        ],
        out_specs=pl.BlockSpec((1, lb, w), lambda bi, j: (bi, j, 0)),
        out_shape=jax.ShapeDtypeStruct((b, s, w), BF16),
        scratch_shapes=[
            pltpu.VMEM((1, n), F32),
            pltpu.VMEM((w // PAIR, PAIR, PAIR), F32),
            big(), big(), big(), big(), big(), big(), big(), big(), big(),
        ],
        compiler_params=pltpu.CompilerParams(
            dimension_semantics=("arbitrary", "arbitrary"), vmem_limit_bytes=VMEM_LIMIT),
        name="rwkv7",
    )(prw, mu, w0, dw2, a0, aw2, gw2, k_k, k_a, r_k, gn_g, gn_b)


def _outproj_body(sb_ref, y_ref, x_ref, wa_ref, wb_ref, g_ref, o_ref):
    m = _dot(sb_ref[...], wa_ref[...]) + _dot(y_ref[...], wb_ref[...])
    o_ref[...] = x_ref[...] + _rms(m, g_ref[...])


def _outproj(sb2d, y2d, x2d, w_out, g, tm):
    t, d = x2d.shape
    const = lambda i: (0, 0)
    return pl.pallas_call(
        _outproj_body,
        grid=(t // tm,),
        in_specs=[
            pl.BlockSpec((tm, SB_WIDTH), lambda i: (i, 0)),
            pl.BlockSpec((tm, RWKV_WIDTH), lambda i: (i, 0)),
            pl.BlockSpec((tm, d), lambda i: (i, 0)),
            pl.BlockSpec((SB_WIDTH, d), lambda i: (0, 0)),
            pl.BlockSpec((RWKV_WIDTH, d), lambda i: (1, 0)),
            pl.BlockSpec((1, d), const),
        ],
        out_specs=pl.BlockSpec((tm, d), lambda i: (i, 0)),
        out_shape=jax.ShapeDtypeStruct((t, d), F32),
        compiler_params=pltpu.CompilerParams(
            dimension_semantics=("arbitrary",), vmem_limit_bytes=VMEM_LIMIT),
        name="outproj",
    )(sb2d, y2d, x2d, w_out, w_out, g)


def _pick(n, pref):
    return pref if n % pref == 0 else n


def kernel(x, ffn1_pre_g, ffn1_post_g, ffn1_w_gate, ffn1_w_up, ffn1_w_down, mix_pre_g, mix_post_g, w_in, shift_mu, sb_out_g, decay_w0, decay_w2, iclr_a0, iclr_a2, gate_w2, k_k, k_a, r_k, gn_g, gn_b, w_out, ffn2_pre_g, ffn2_post_g, ffn2_w_gate, ffn2_w_up, ffn2_w_down):
    b, s, d = x.shape
    depth = ffn1_pre_g.shape[0]
    t = b * s
    tm = _pick(t, 512)
    blk = _pick(s, 256)
    lb = _pick(s, 256)
    x2d = x.reshape(t, d)
    for l in range(depth):
        x2d = _ffn(x2d, ffn1_pre_g[l][None], ffn1_post_g[l][None], ffn1_w_gate[l].astype(BF16),
                   ffn1_w_up[l].astype(BF16), ffn1_w_down[l].astype(BF16), tm)
        qkv, prw = _inproj(x2d, mix_pre_g[l][None], w_in[l].astype(BF16), tm)
        sb = _sb_attention(qkv.reshape(b, s, -1), sb_out_g[l][None], blk)
        zpad = jnp.zeros((AAA_LORA, RWKV_WIDTH), F32)
        dw2 = jnp.concatenate([decay_w2[l], zpad], axis=0).astype(BF16)
        aw2 = jnp.concatenate([zpad, iclr_a2[l]], axis=0).astype(BF16)
        y = _rwkv(prw.reshape(b, s, -1), shift_mu[l][None], decay_w0[l][None], dw2,
                  iclr_a0[l][None], aw2, gate_w2[l].astype(BF16), k_k[l][None], k_a[l][None],
                  r_k[l].reshape(1, -1), gn_g[l][None], gn_b[l][None], lb)
        x2d = _outproj(sb.reshape(t, -1), y.reshape(t, -1), x2d, w_out[l].astype(BF16),
                       mix_post_g[l][None], tm)
        x2d = _ffn(x2d, ffn2_pre_g[l][None], ffn2_post_g[l][None], ffn2_w_gate[l].astype(BF16),
                   ffn2_w_up[l].astype(BF16), ffn2_w_down[l].astype(BF16), tm)
    return x2d.reshape(b, s, d)
```

```python
import functools

import jax
import jax.numpy as jnp
from jax import lax
from jax.experimental import pallas as pl
from jax.experimental.pallas import tpu as pltpu

F32 = jnp.float32
BF16 = jnp.bfloat16

HEAD_DIM = 64
SB_HEADS = 8
RWKV_HEADS = 8
SB_WIDTH = SB_HEADS * HEAD_DIM
RWKV_WIDTH = RWKV_HEADS * HEAD_DIM
DECAY_LORA = 64
AAA_LORA = 64
GATE_LORA = 128
NORM_EPS = 1e-6
GN_EPS = HEAD_DIM * 1e-5

LANES = 128
PAIR = LANES
GROUP = 2 * LANES
LOG2E = 1.4426950408889634
CHUNK = 64
VMEM_LIMIT = 56 * 1024 * 1024

_NT = (((1,), (1,)), ((), ()))
_TN = (((0,), (0,)), ((), ()))


def _dot(a, b, dims=None):
    if dims is None:
        dims = (((a.ndim - 1,), (0,)), ((), ()))
    return lax.dot_general(a, b, dims, preferred_element_type=F32)


def _rms(x, g):
    return x * lax.rsqrt(jnp.mean(x * x, axis=-1, keepdims=True) + NORM_EPS) * g


def _softplus(x):
    return jnp.maximum(x, 0.0) + jnp.log1p(jnp.exp(-jnp.abs(x)))


def _sigmoid(x):
    return 1.0 / (1.0 + jnp.exp(-x))


def _split(x):
    hi = x.astype(BF16)
    return hi, (x - hi.astype(F32)).astype(BF16)


def _dot_exact_lhs(m, x):
    hi, lo = _split(x)
    return _dot(m, hi) + _dot(m, lo)


def _head_sums(x):
    r = lax.broadcasted_iota(jnp.int32, (GROUP, GROUP), 0) // HEAD_DIM
    c = lax.broadcasted_iota(jnp.int32, (GROUP, GROUP), 1) // HEAD_DIM
    ones = (r == c).astype(BF16)
    parts = []
    for gi in range(x.shape[1] // GROUP):
        hi, lo = _split(x[:, gi * GROUP:(gi + 1) * GROUP])
        parts.append(_dot(hi, ones) + _dot(lo, ones))
    return jnp.concatenate(parts, axis=1)


def _ffn_body(x_ref, pre_ref, post_ref, wg_ref, wu_ref, wd_ref, o_ref):
    x = x_ref[...]
    xn = _rms(x, pre_ref[...]).astype(BF16)
    g = _dot(xn, wg_ref[...])
    u = _dot(xn, wu_ref[...])
    h = (g * _sigmoid(g) * u).astype(BF16)
    f = _dot(h, wd_ref[...])
    o_ref[...] = x + 0.5 * _rms(f, post_ref[...])


def _ffn(x2d, pre_g, post_g, wg, wu, wd, tm):
    t, d = x2d.shape
    dff = wg.shape[1]
    const = lambda i: (0, 0)
    return pl.pallas_call(
        _ffn_body,
        grid=(t // tm,),
        in_specs=[
            pl.BlockSpec((tm, d), lambda i: (i, 0)),
            pl.BlockSpec((1, d), const),
            pl.BlockSpec((1, d), const),
            pl.BlockSpec((d, dff), const, pipeline_mode=pl.Buffered(1)),
            pl.BlockSpec((d, dff), const, pipeline_mode=pl.Buffered(1)),
            pl.BlockSpec((dff, d), const, pipeline_mode=pl.Buffered(1)),
        ],
        out_specs=pl.BlockSpec((tm, d), lambda i: (i, 0)),
        out_shape=jax.ShapeDtypeStruct((t, d), F32),
        compiler_params=pltpu.CompilerParams(
            dimension_semantics=("arbitrary",), vmem_limit_bytes=VMEM_LIMIT),
        name="ffn",
    )(x2d, pre_g, post_g, wg, wu, wd)


def _inproj_body(x_ref, g_ref, w_ref, q_ref, k_ref, vt_ref, prw_ref):
    xn = _rms(x_ref[...], g_ref[...]).astype(BF16)
    p = _dot(xn, w_ref[...])
    q_ref[...] = (p[:, :SB_WIDTH] * (HEAD_DIM ** -0.5 * LOG2E)).astype(BF16)
    k_ref[...] = p[:, SB_WIDTH:2 * SB_WIDTH].astype(BF16)
    vt_ref[0] = p[:, 2 * SB_WIDTH:3 * SB_WIDTH].T.astype(BF16)
    prw_ref[...] = p[:, 3 * SB_WIDTH:]


def _inproj(x2d, g, w_in, tm, seq):
    t, d = x2d.shape
    n = w_in.shape[1]
    nrw = n - 3 * SB_WIDTH
    per_seq = seq // tm
    const = lambda i: (0, 0)
    row = lambda i: (i, 0)
    return pl.pallas_call(
        _inproj_body,
        grid=(t // tm,),
        in_specs=[
            pl.BlockSpec((tm, d), row),
            pl.BlockSpec((1, d), const),
            pl.BlockSpec((d, n), const, pipeline_mode=pl.Buffered(1)),
        ],
        out_specs=[
            pl.BlockSpec((tm, SB_WIDTH), row),
            pl.BlockSpec((tm, SB_WIDTH), row),
            pl.BlockSpec((1, SB_WIDTH, tm), lambda i: (i // per_seq, 0, i % per_seq)),
            pl.BlockSpec((tm, nrw), row),
        ],
        out_shape=[
            jax.ShapeDtypeStruct((t, SB_WIDTH), BF16),
            jax.ShapeDtypeStruct((t, SB_WIDTH), BF16),
            jax.ShapeDtypeStruct((t // seq, SB_WIDTH, seq), BF16),
            jax.ShapeDtypeStruct((t, nrw), F32),
        ],
        compiler_params=pltpu.CompilerParams(
            dimension_semantics=("arbitrary",), vmem_limit_bytes=VMEM_LIMIT),
        name="inproj",
    )(x2d, g, w_in)


def _sb_body(q_ref, k_ref, vt_ref, g_ref, o_ref, acc_ref, *, blk):
    qi = pl.program_id(2)
    nh = GROUP // HEAD_DIM
    q = q_ref[...]
    lane = lax.broadcasted_iota(jnp.int32, (blk, GROUP), 1)
    zero = jnp.zeros_like(q)
    q_heads = [jnp.where(lane // HEAD_DIM == h, q, zero) for h in range(nh)]
    key = lax.broadcasted_iota(jnp.int32, (blk, blk), 0)
    qry = lax.broadcasted_iota(jnp.int32, (blk, blk), 1)
    neg_suffix = jnp.where(qry >= key, -1.0, 0.0).astype(BF16)
    causal = key < qry

    def visit(kbs, carries, diagonal=False):
        carries = list(carries)
        starts = [pl.multiple_of(kb * blk, blk) for kb in kbs]
        kblks = [k_ref[pl.ds(st, blk), :] for st in starts]
        vts = [[vt_ref[0, h * HEAD_DIM:(h + 1) * HEAD_DIM, pl.ds(st, blk)] for h in range(nh)]
               for st in starts]
        jobs = [(i, h) for i in range(len(kbs)) for h in range(nh)]
        z = {ih: _dot(kblks[ih[0]], q_heads[ih[1]], _NT) for ih in jobs}
        sp = {}
        for ih in jobs:
            neg_abs = pltpu.bitcast(
                pltpu.bitcast(z[ih], jnp.uint32) | jnp.uint32(0x80000000), F32)
            s = jnp.maximum(z[ih], 0.0) + jnp.log(1.0 + jnp.exp2(neg_abs)) * LOG2E
            if diagonal:
                s = jnp.where(causal, s, 0.0)
            sp[ih] = s.astype(BF16)
        csum = {ih: _dot(neg_suffix, sp[ih]) for ih in jobs}
        w = {}
        for ih in jobs:
            c = csum[ih] + carries[ih[1]]
            carries[ih[1]] = c[0:1, :]
            x = jnp.exp2(z[ih] + c)
            if diagonal:
                x = jnp.where(causal, x, 0.0)
            w[ih] = x.astype(BF16)
        outs = {ih: _dot(vts[ih[0]][ih[1]], w[ih]) for ih in jobs}
        for h in range(nh):
            acc_ref[h] += sum(outs[(i, h)] for i in range(len(kbs)))
        return tuple(carries)

    acc_ref[...] = jnp.zeros_like(acc_ref)
    zc = jnp.zeros((1, blk), F32)
    carries = visit([qi], (zc,) * nh, diagonal=True)
    odd = qi % 2
    carries = lax.cond(odd == 1, lambda cs: visit([qi - 1], cs), lambda cs: cs, carries)
    top = qi - 1 - odd
    lax.fori_loop(0, qi // 2, lambda i, cs: visit([top - 2 * i, top - 2 * i - 1], cs), carries)

    parts = []
    for h in range(nh):
        a = acc_ref[h]
        ms = jnp.mean(a * a, axis=0, keepdims=True)
        parts.append(a * lax.rsqrt(ms + NORM_EPS))
    out = jnp.concatenate(parts, axis=0).T
    o_ref[...] = (out * g_ref[...]).astype(BF16)


def _sb_attention(q, k, vt, sb_g, blk):
    t = q.shape[0]
    b, _, s = vt.shape
    ngroup = SB_WIDTH // GROUP
    nq = s // blk
    return pl.pallas_call(
        functools.partial(_sb_body, blk=blk),
        grid=(b, ngroup, nq),
        in_specs=[
            pl.BlockSpec((blk, GROUP), lambda bi, g, qi: (bi * nq + qi, g)),
            pl.BlockSpec((s, GROUP), lambda bi, g, qi: (bi, g)),
            pl.BlockSpec((1, GROUP, s), lambda bi, g, qi: (bi, g, 0)),
            pl.BlockSpec((1, GROUP), lambda bi, g, qi: (0, g)),
        ],
        out_specs=pl.BlockSpec((blk, GROUP), lambda bi, g, qi: (bi * nq + qi, g)),
        out_shape=jax.ShapeDtypeStruct((t, SB_WIDTH), BF16),
        scratch_shapes=[pltpu.VMEM((GROUP // HEAD_DIM, HEAD_DIM, blk), F32)],
        compiler_params=pltpu.CompilerParams(
            dimension_semantics=("arbitrary", "arbitrary", "arbitrary"),
            vmem_limit_bytes=VMEM_LIMIT),
        name="sb_attention",
    )(q, k, vt, sb_g)


def _rwkv_body(p_ref, mu_ref, w0_ref, dw2_ref, a0_ref, aw2_ref, gw2_ref, kk_ref, ka_ref,
               rk_ref, gng_ref, gnb_ref, o_ref,
               last_ref, state_ref, rt_ref, at_ref, bt_ref, kt_ref, v_ref, bh_ref, kh_ref,
               wc_ref, y_ref, *, lb):
    j = pl.program_id(1)
    w = RWKV_WIDTH
    npair = w // PAIR

    @pl.when(j == 0)
    def _():
        last_ref[...] = jnp.zeros_like(last_ref)
        state_ref[...] = jnp.zeros_like(state_ref)

    p = p_ref[0]
    rows = lax.broadcasted_iota(jnp.int32, p.shape, 0)
    prev = jnp.where(rows == 0, last_ref[...], pltpu.roll(p, 1, 0))
    last_ref[...] = p[lb - 1:lb, :]
    xs = p + (prev - p) * mu_ref[...]
    r = xs[:, 0:w]
    kr = xs[:, w:2 * w]
    vr = xs[:, 2 * w:3 * w]
    lora_in = xs[:, 3 * w:3 * w + DECAY_LORA + AAA_LORA]
    gd = xs[:, 3 * w + DECAY_LORA + AAA_LORA:]
    lora_w = _dot(jnp.tanh(lora_in).astype(BF16), dw2_ref[...])
    lora_a = _dot(lora_in.astype(BF16), aw2_ref[...])
    gate = _dot(_sigmoid(gd).astype(BF16), gw2_ref[...])
    w_log = -_softplus(-(w0_ref[...] + lora_w)) - 0.5
    lw = -jnp.exp(w_log)
    a = _sigmoid(a0_ref[...] + lora_a)
    kk = kr * kk_ref[...]
    k2 = kr * (1.0 + (a - 1.0) * ka_ref[...])
    bonus = _head_sums(r * k2 * rk_ref[...]) * vr
    kkn = kk * lax.rsqrt(jnp.maximum(_head_sums(kk * kk), 1e-24))
    a_vec = -kkn
    b_vec = kkn * a

    tr = lax.broadcasted_iota(jnp.int32, (lb, lb), 0)
    tc = lax.broadcasted_iota(jnp.int32, (lb, lb), 1)
    prefix = ((tr // CHUNK) == (tc // CHUNK)) & (tc <= tr)
    cum = _dot_exact_lhs(prefix.astype(BF16), lw)
    tot = jnp.concatenate(
        [jnp.broadcast_to(cum[c * CHUNK + CHUNK - 1:(c + 1) * CHUNK, :], (CHUNK, w))
         for c in range(lb // CHUNK)], axis=0)
    dec_in = jnp.exp(cum)
    dec_out = jnp.exp(-cum)
    dec_rest = jnp.exp(tot - cum)
    rt_ref[...] = r * dec_in
    at_ref[...] = a_vec * jnp.exp(cum - lw)
    bt_ref[...] = b_vec * dec_out
    kt_ref[...] = k2 * dec_out
    v_ref[...] = vr
    bh_ref[...] = b_vec * dec_rest
    kh_ref[...] = k2 * dec_rest
    wc_ref[...] = jnp.exp(tot)

    n2 = 2 * CHUNK
    lane = lax.broadcasted_iota(jnp.int32, (CHUNK, PAIR), 1)
    first = lane < HEAD_DIM
    br = lax.broadcasted_iota(jnp.int32, (n2, n2), 0)
    bc = lax.broadcasted_iota(jnp.int32, (n2, n2), 1)
    same_head = (br // CHUNK) == (bc // CHUNK)
    strict = same_head & (bc < br)
    incl = same_head & (bc <= br)
    eye = br == bc

    def stack_masked(x):
        return jnp.concatenate([jnp.where(first, x, 0.0), jnp.where(first, 0.0, x)], axis=0)

    def stack(x):
        return jnp.concatenate([x, x], axis=0)

    def chunk_step(c, carry):
        t0 = pl.multiple_of(c * CHUNK, CHUNK)
        sls = [(pl.ds(t0, CHUNK), slice(pi * PAIR, (pi + 1) * PAIR)) for pi in range(npair)]
        each = lambda f: [f(pi) for pi in range(npair)]
        a_sm = each(lambda pi: stack_masked(at_ref[sls[pi]]))
        r_sm = each(lambda pi: stack_masked(rt_ref[sls[pi]]))
        v_sm = each(lambda pi: stack_masked(v_ref[sls[pi]]).astype(BF16))
        bh_sm = each(lambda pi: stack_masked(bh_ref[sls[pi]]).astype(BF16))
        kh_sm = each(lambda pi: stack_masked(kh_ref[sls[pi]]).astype(BF16))
        wc = each(lambda pi: wc_ref[pl.ds(t0, 8), sls[pi][1]][0:1, :])
        s16 = each(lambda pi: state_ref[pi].astype(BF16))
        mm = each(lambda pi: _dot(
            jnp.concatenate([a_sm[pi], r_sm[pi]], axis=0).astype(BF16),
            jnp.concatenate([stack(bt_ref[sls[pi]]), stack(kt_ref[sls[pi]])], axis=0).astype(BF16),
            _NT))
        m_ab = each(lambda pi: jnp.where(strict, mm[pi][:n2, :n2], 0.0))
        m_ak = each(lambda pi: jnp.where(strict, mm[pi][:n2, n2:], 0.0).astype(BF16))
        m_rbk = each(lambda pi: jnp.concatenate(
            [jnp.where(incl, mm[pi][n2:, :n2], 0.0), jnp.where(incl, mm[pi][n2:, n2:], 0.0)],
            axis=1).astype(BF16))
        x = each(lambda pi: jnp.concatenate([a_sm[pi], _dot(m_ak[pi], v_sm[pi])], axis=1))
        lp = m_ab
        steps = CHUNK.bit_length() - 1
        for i in range(steps):
            lp16 = each(lambda pi: lp[pi].astype(BF16))
            if i < steps - 1:
                both = each(lambda pi: _dot(
                    lp16[pi], jnp.concatenate([lp16[pi], x[pi].astype(BF16)], axis=1)))
                lp = each(lambda pi: both[pi][:, :n2])
                x = each(lambda pi: x[pi] + both[pi][:, n2:])
            else:
                x = each(lambda pi: x[pi] + _dot(lp16[pi], x[pi].astype(BF16)))
        pq = each(lambda pi: x[pi].astype(BF16))
        zeros = jnp.zeros((n2, PAIR), BF16)
        gy = each(lambda pi: _dot(m_rbk[pi], jnp.concatenate(
            [pq[pi], jnp.concatenate([zeros, v_sm[pi]], axis=1)], axis=0)))
        phi = each(lambda pi: _dot(pq[pi][:, :PAIR], bh_sm[pi], _TN) + jnp.where(eye, wc[pi], 0.0))
        psi = each(lambda pi: _dot(jnp.concatenate([pq[pi][:, PAIR:], v_sm[pi]], axis=0),
                                   jnp.concatenate([bh_sm[pi], kh_sm[pi]], axis=0), _TN))
        g_pk = each(lambda pi: (r_sm[pi][:CHUNK] + gy[pi][:CHUNK, :PAIR]
                                + r_sm[pi][CHUNK:] + gy[pi][CHUNK:, :PAIR]).astype(BF16))
        y = each(lambda pi: _dot(g_pk[pi], s16[pi], _NT)
                 + gy[pi][:CHUNK, PAIR:] + gy[pi][CHUNK:, PAIR:])
        s_new = each(lambda pi: _dot(s16[pi], phi[pi].astype(BF16)) + psi[pi])
        for pi in range(npair):
            y_ref[sls[pi]] = y[pi]
            state_ref[pi] = s_new[pi]
        return carry

    lax.fori_loop(0, lb // CHUNK, chunk_step, 0)

    y = y_ref[...]
    inv = 1.0 / HEAD_DIM
    d = y - _head_sums(y) * inv
    var = _head_sums(d * d) * inv
    yn = d * lax.rsqrt(var + GN_EPS) * gng_ref[...] + gnb_ref[...]
    o_ref[0] = ((yn + bonus) * gate).astype(BF16)


def _rwkv(prw, mu, w0, dw2, a0, aw2, gw2, k_k, k_a, r_k, gn_g, gn_b, lb):
    b, s, n = prw.shape
    w = RWKV_WIDTH
    const = lambda bi, j: (0, 0)
    vec = lambda m: pl.BlockSpec((1, m), const)
    big = lambda: pltpu.VMEM((lb, w), F32)
    return pl.pallas_call(
        functools.partial(_rwkv_body, lb=lb),
        grid=(b, s // lb),
        in_specs=[
            pl.BlockSpec((1, lb, n), lambda bi, j: (bi, j, 0)),
            vec(n), vec(w),
            pl.BlockSpec((DECAY_LORA + AAA_LORA, w), const),
            vec(w),
            pl.BlockSpec((DECAY_LORA + AAA_LORA, w), const),
            pl.BlockSpec((GATE_LORA, w), const),
            vec(w), vec(w), vec(w), vec(w), vec(w),
        ],
        out_specs=pl.BlockSpec((1, lb, w), lambda bi, j: (bi, j, 0)),
        out_shape=jax.ShapeDtypeStruct((b, s, w), BF16),
        scratch_shapes=[
            pltpu.VMEM((1, n), F32),
            pltpu.VMEM((w // PAIR, PAIR, PAIR), F32),
            big(), big(), big(), big(), big(), big(), big(), big(), big(),
        ],
        compiler_params=pltpu.CompilerParams(
            dimension_semantics=("arbitrary", "arbitrary"), vmem_limit_bytes=VMEM_LIMIT),
        name="rwkv7",
    )(prw, mu, w0, dw2, a0, aw2, gw2, k_k, k_a, r_k, gn_g, gn_b)


def _outproj_body(sb_ref, y_ref, x_ref, wa_ref, wb_ref, g_ref, o_ref):
    m = _dot(sb_ref[...], wa_ref[...]) + _dot(y_ref[...], wb_ref[...])
    o_ref[...] = x_ref[...] + _rms(m, g_ref[...])


def _outproj(sb2d, y2d, x2d, w_out, g, tm):
    t, d = x2d.shape
    const = lambda i: (0, 0)
    return pl.pallas_call(
        _outproj_body,
        grid=(t // tm,),
        in_specs=[
            pl.BlockSpec((tm, SB_WIDTH), lambda i: (i, 0)),
            pl.BlockSpec((tm, RWKV_WIDTH), lambda i: (i, 0)),
            pl.BlockSpec((tm, d), lambda i: (i, 0)),
            pl.BlockSpec((SB_WIDTH, d), lambda i: (0, 0)),
            pl.BlockSpec((RWKV_WIDTH, d), lambda i: (1, 0)),
            pl.BlockSpec((1, d), const),
        ],
        out_specs=pl.BlockSpec((tm, d), lambda i: (i, 0)),
        out_shape=jax.ShapeDtypeStruct((t, d), F32),
        compiler_params=pltpu.CompilerParams(
            dimension_semantics=("arbitrary",), vmem_limit_bytes=VMEM_LIMIT),
        name="outproj",
    )(sb2d, y2d, x2d, w_out, w_out, g)


def _pick(n, pref):
    return pref if n % pref == 0 else n


def kernel(x, ffn1_pre_g, ffn1_post_g, ffn1_w_gate, ffn1_w_up, ffn1_w_down, mix_pre_g, mix_post_g, w_in, shift_mu, sb_out_g, decay_w0, decay_w2, iclr_a0, iclr_a2, gate_w2, k_k, k_a, r_k, gn_g, gn_b, w_out, ffn2_pre_g, ffn2_post_g, ffn2_w_gate, ffn2_w_up, ffn2_w_down):
    b, s, d = x.shape
    depth = ffn1_pre_g.shape[0]
    t = b * s
    tm = _pick(s, 512)
    blk = _pick(s, 256)
    lb = _pick(s, 256)
    x2d = x.reshape(t, d)
    for l in range(depth):
        x2d = _ffn(x2d, ffn1_pre_g[l][None], ffn1_post_g[l][None], ffn1_w_gate[l].astype(BF16),
                   ffn1_w_up[l].astype(BF16), ffn1_w_down[l].astype(BF16), tm)
        q, k, vt, prw = _inproj(x2d, mix_pre_g[l][None], w_in[l].astype(BF16), tm, s)
        sb = _sb_attention(q, k, vt, sb_out_g[l][None], blk)
        zpad = jnp.zeros((AAA_LORA, RWKV_WIDTH), F32)
        dw2 = jnp.concatenate([decay_w2[l], zpad], axis=0).astype(BF16)
        aw2 = jnp.concatenate([zpad, iclr_a2[l]], axis=0).astype(BF16)
        y = _rwkv(prw.reshape(b, s, -1), shift_mu[l][None], decay_w0[l][None], dw2,
                  iclr_a0[l][None], aw2, gate_w2[l].astype(BF16), k_k[l][None], k_a[l][None],
                  r_k[l].reshape(1, -1), gn_g[l][None], gn_b[l][None], lb)
        x2d = _outproj(sb, y.reshape(t, -1), x2d, w_out[l].astype(BF16), mix_post_g[l][None], tm)
        x2d = _ffn(x2d, ffn2_pre_g[l][None], ffn2_post_g[l][None], ffn2_w_gate[l].astype(BF16),
                   ffn2_w_up[l].astype(BF16), ffn2_w_down[l].astype(BF16), tm)
    return x2d.reshape(b, s, d)
```

```python
import functools

import jax
import jax.numpy as jnp
from jax import lax
from jax.experimental import pallas as pl
from jax.experimental.pallas import tpu as pltpu

F32 = jnp.float32
BF16 = jnp.bfloat16

HEAD_DIM = 64
SB_HEADS = 8
RWKV_HEADS = 8
SB_WIDTH = SB_HEADS * HEAD_DIM
RWKV_WIDTH = RWKV_HEADS * HEAD_DIM
DECAY_LORA = 64
AAA_LORA = 64
GATE_LORA = 128
NORM_EPS = 1e-6
GN_EPS = HEAD_DIM * 1e-5

LANES = 128
PAIR = LANES
GROUP = 2 * LANES
LOG2E = 1.4426950408889634
CHUNK = 64
VMEM_LIMIT = 56 * 1024 * 1024

_NT = (((1,), (1,)), ((), ()))
_TN = (((0,), (0,)), ((), ()))


def _dot(a, b, dims=None):
    if dims is None:
        dims = (((a.ndim - 1,), (0,)), ((), ()))
    return lax.dot_general(a, b, dims, preferred_element_type=F32)


def _rms(x, g):
    return x * lax.rsqrt(jnp.mean(x * x, axis=-1, keepdims=True) + NORM_EPS) * g


def _softplus(x):
    return jnp.maximum(x, 0.0) + jnp.log(1.0 + jnp.exp(-jnp.abs(x)))


def _sigmoid(x):
    return 1.0 / (1.0 + jnp.exp(-x))


def _split(x):
    hi = x.astype(BF16)
    return hi, (x - hi.astype(F32)).astype(BF16)


def _head_sums(x):
    r = lax.broadcasted_iota(jnp.int32, (GROUP, GROUP), 0) // HEAD_DIM
    c = lax.broadcasted_iota(jnp.int32, (GROUP, GROUP), 1) // HEAD_DIM
    ones = (r == c).astype(BF16)
    parts = [_dot(x[:, gi * GROUP:(gi + 1) * GROUP].astype(BF16), ones)
             for gi in range(x.shape[1] // GROUP)]
    return jnp.concatenate(parts, axis=1)


def _ffn_body(x_ref, pre_ref, post_ref, wg_ref, wu_ref, wd_ref, o_ref):
    x = x_ref[...]
    xn = _rms(x, pre_ref[...]).astype(BF16)
    g = _dot(xn, wg_ref[...])
    u = _dot(xn, wu_ref[...])
    h = (g * _sigmoid(g) * u).astype(BF16)
    f = _dot(h, wd_ref[...])
    o_ref[...] = x + 0.5 * _rms(f, post_ref[...])


def _ffn(x2d, pre_g, post_g, wg, wu, wd, tm):
    t, d = x2d.shape
    dff = wg.shape[1]
    const = lambda i: (0, 0)
    return pl.pallas_call(
        _ffn_body,
        grid=(t // tm,),
        in_specs=[
            pl.BlockSpec((tm, d), lambda i: (i, 0)),
            pl.BlockSpec((1, d), const),
            pl.BlockSpec((1, d), const),
            pl.BlockSpec((d, dff), const, pipeline_mode=pl.Buffered(1)),
            pl.BlockSpec((d, dff), const, pipeline_mode=pl.Buffered(1)),
            pl.BlockSpec((dff, d), const, pipeline_mode=pl.Buffered(1)),
        ],
        out_specs=pl.BlockSpec((tm, d), lambda i: (i, 0)),
        out_shape=jax.ShapeDtypeStruct((t, d), F32),
        compiler_params=pltpu.CompilerParams(
            dimension_semantics=("arbitrary",), vmem_limit_bytes=VMEM_LIMIT),
        name="ffn",
    )(x2d, pre_g, post_g, wg, wu, wd)


def _inproj_body(x_ref, g_ref, w_ref, q_ref, k_ref, vt_ref, prw_ref):
    xn = _rms(x_ref[...], g_ref[...]).astype(BF16)
    p = _dot(xn, w_ref[...])
    q_ref[...] = (p[:, :SB_WIDTH] * (HEAD_DIM ** -0.5 * LOG2E)).astype(BF16)
    k_ref[...] = p[:, SB_WIDTH:2 * SB_WIDTH].astype(BF16)
    vt_ref[0] = p[:, 2 * SB_WIDTH:3 * SB_WIDTH].T.astype(BF16)
    prw_ref[...] = p[:, 3 * SB_WIDTH:]


def _inproj(x2d, g, w_in, tm, seq):
    t, d = x2d.shape
    n = w_in.shape[1]
    nrw = n - 3 * SB_WIDTH
    per_seq = seq // tm
    const = lambda i: (0, 0)
    row = lambda i: (i, 0)
    return pl.pallas_call(
        _inproj_body,
        grid=(t // tm,),
        in_specs=[
            pl.BlockSpec((tm, d), row),
            pl.BlockSpec((1, d), const),
            pl.BlockSpec((d, n), const, pipeline_mode=pl.Buffered(1)),
        ],
        out_specs=[
            pl.BlockSpec((tm, SB_WIDTH), row),
            pl.BlockSpec((tm, SB_WIDTH), row),
            pl.BlockSpec((1, SB_WIDTH, tm), lambda i: (i // per_seq, 0, i % per_seq)),
            pl.BlockSpec((tm, nrw), row),
        ],
        out_shape=[
            jax.ShapeDtypeStruct((t, SB_WIDTH), BF16),
            jax.ShapeDtypeStruct((t, SB_WIDTH), BF16),
            jax.ShapeDtypeStruct((t // seq, SB_WIDTH, seq), BF16),
            jax.ShapeDtypeStruct((t, nrw), F32),
        ],
        compiler_params=pltpu.CompilerParams(
            dimension_semantics=("arbitrary",), vmem_limit_bytes=VMEM_LIMIT),
        name="inproj",
    )(x2d, g, w_in)


def _sb_body(q_ref, k_ref, vt_ref, g_ref, o_ref, acc_ref, qh_ref,
             zraw0, craw0, z0, sp0, zraw1, craw1, z1, sp1, *, blk):
    qi = pl.program_id(2)
    nh = GROUP // HEAD_DIM
    halves = (tuple(range(nh // 2)), tuple(range(nh // 2, nh)))
    bufs = (dict(zraw=zraw0, craw=craw0, z=z0, sp=sp0),
            dict(zraw=zraw1, craw=craw1, z=z1, sp=sp1))

    q = q_ref[...]
    lane = lax.broadcasted_iota(jnp.int32, (blk, GROUP), 1)
    zero = jnp.zeros_like(q)
    for h in range(nh):
        qh_ref[h] = jnp.where(lane // HEAD_DIM == h, q, zero)
    key = lax.broadcasted_iota(jnp.int32, (blk, blk), 0)
    qry = lax.broadcasted_iota(jnp.int32, (blk, blk), 1)
    neg_suffix = jnp.where(qry >= key, -1.0, 0.0).astype(BF16)
    causal = key < qry

    def rows_of(kb):
        return pl.ds(pl.multiple_of(kb * blk, blk), blk)

    def part(ew=None, mx=None):
        if mx is not None:
            mhalf, kb_s, do_sums = mx
            mb = bufs[mhalf]
            if do_sums:
                sums = [_dot(neg_suffix, mb["sp"][i]) for i in range(len(halves[mhalf]))]
            if kb_s is not None:
                kblk = k_ref[rows_of(kb_s), :]
                raw = [_dot(kblk, qh_ref[h], _NT) for h in halves[mhalf]]
        new = None
        if ew is not None:
            ehalf, kb_w, do_score, diagonal, carries = ew
            eb = bufs[ehalf]
            new = list(carries)
            outs = []
            if kb_w is not None:
                for i, h in enumerate(halves[ehalf]):
                    c = eb["craw"][i]
                    new[h] = c[0:1, :] + carries[h]
                    w = jnp.exp2(eb["z"][i] + c).astype(BF16)
                    vt = vt_ref[0, h * HEAD_DIM:(h + 1) * HEAD_DIM, rows_of(kb_w)]
                    outs.append(_dot(vt, w) * jnp.exp2(carries[h]))
            if do_score:
                for i in range(len(halves[ehalf])):
                    z = eb["zraw"][i]
                    if diagonal:
                        z = jnp.where(causal, z, -1e30)
                    sp = jnp.maximum(z, 0.0) + jnp.log(1.0 + jnp.exp2(-jnp.abs(z))) * LOG2E
                    eb["z"][i] = z
                    eb["sp"][i] = sp.astype(BF16)
        if mx is not None:
            for i in range(len(halves[mhalf])):
                if do_sums:
                    mb["craw"][i] = sums[i]
                if kb_s is not None:
                    mb["zraw"][i] = raw[i]
        if ew is not None and kb_w is not None:
            for i, h in enumerate(halves[ehalf]):
                acc_ref[h] += outs[i]
        return None if new is None else tuple(new)

    acc_ref[...] = jnp.zeros_like(acc_ref)
    carries = (jnp.zeros((1, blk), F32),) * nh
    part(mx=(0, qi, False))
    part(ew=(0, None, True, True, carries), mx=(1, qi, False))
    part(ew=(1, None, True, True, carries), mx=(0, jnp.maximum(qi - 1, 0), True))

    def sweep(s, carries):
        kb_w = qi - s + 1
        carries = part(ew=(0, kb_w, True, False, carries), mx=(1, kb_w - 1, True))
        return part(ew=(1, kb_w, True, False, carries),
                    mx=(0, jnp.maximum(kb_w - 2, 0), True))

    carries = lax.fori_loop(1, qi + 1, sweep, carries)
    carries = part(ew=(0, 0, False, False, carries), mx=(1, None, True))
    part(ew=(1, 0, False, False, carries))

    parts = []
    for h in range(nh):
        a = acc_ref[h]
        ms = jnp.mean(a * a, axis=0, keepdims=True)
        parts.append(a * lax.rsqrt(ms + NORM_EPS))
    out = jnp.concatenate(parts, axis=0).T
    o_ref[...] = (out * g_ref[...]).astype(BF16)


def _sb_attention(q, k, vt, sb_g, blk):
    t = q.shape[0]
    b, _, s = vt.shape
    ngroup = SB_WIDTH // GROUP
    nq = s // blk
    nh = GROUP // HEAD_DIM
    half_f32 = pltpu.VMEM((nh // 2, blk, blk), F32)
    half_bf16 = pltpu.VMEM((nh // 2, blk, blk), BF16)
    return pl.pallas_call(
        functools.partial(_sb_body, blk=blk),
        grid=(b, ngroup, nq),
        in_specs=[
            pl.BlockSpec((blk, GROUP), lambda bi, g, qi: (bi * nq + qi, g)),
            pl.BlockSpec((s, GROUP), lambda bi, g, qi: (bi, g)),
            pl.BlockSpec((1, GROUP, s), lambda bi, g, qi: (bi, g, 0)),
            pl.BlockSpec((1, GROUP), lambda bi, g, qi: (0, g)),
        ],
        out_specs=pl.BlockSpec((blk, GROUP), lambda bi, g, qi: (bi * nq + qi, g)),
        out_shape=jax.ShapeDtypeStruct((t, SB_WIDTH), BF16),
        scratch_shapes=[
            pltpu.VMEM((nh, HEAD_DIM, blk), F32),
            pltpu.VMEM((nh, blk, GROUP), BF16),
            half_f32, half_f32, half_f32, half_bf16,
            half_f32, half_f32, half_f32, half_bf16,
        ],
        compiler_params=pltpu.CompilerParams(
            dimension_semantics=("arbitrary", "arbitrary", "arbitrary"),
            vmem_limit_bytes=VMEM_LIMIT),
        name="sb_attention",
    )(q, k, vt, sb_g)


def _rwkv_body(p_ref, mu_ref, w0_ref, dw2_ref, a0_ref, aw2_ref, gw2_ref, kk_ref, ka_ref,
               rk_ref, gng_ref, gnb_ref, o_ref,
               last_ref, state_ref, rt_ref, at_ref, bt_ref, kt_ref, v_ref, bh_ref, kh_ref,
               wc_ref, y_ref, *, lb):
    j = pl.program_id(1)
    w = RWKV_WIDTH
    npair = w // PAIR

    @pl.when(j == 0)
    def _():
        last_ref[...] = jnp.zeros_like(last_ref)
        state_ref[...] = jnp.zeros_like(state_ref)

    p = p_ref[0]
    rows = lax.broadcasted_iota(jnp.int32, p.shape, 0)
    prev = jnp.where(rows == 0, last_ref[...], pltpu.roll(p, 1, 0))
    last_ref[...] = p[lb - 1:lb, :]
    xs = p + (prev - p) * mu_ref[...]
    r = xs[:, 0:w]
    kr = xs[:, w:2 * w]
    vr = xs[:, 2 * w:3 * w]
    lora_in = xs[:, 3 * w:3 * w + DECAY_LORA + AAA_LORA]
    gd = xs[:, 3 * w + DECAY_LORA + AAA_LORA:]
    lora_w = _dot(jnp.tanh(lora_in).astype(BF16), dw2_ref[...])
    lora_a = _dot(lora_in.astype(BF16), aw2_ref[...])
    gate = _dot(_sigmoid(gd).astype(BF16), gw2_ref[...])
    w_log = -_softplus(-(w0_ref[...] + lora_w)) - 0.5
    lw2 = jnp.exp(w_log) * (-LOG2E)
    a = _sigmoid(a0_ref[...] + lora_a)
    kk = kr * kk_ref[...]
    k2 = kr * (1.0 + (a - 1.0) * ka_ref[...])
    bonus = _head_sums(r * k2 * rk_ref[...]) * vr
    kkn = kk * lax.rsqrt(jnp.maximum(_head_sums(kk * kk), 1e-24))
    a_vec = -kkn
    b_vec = kkn * a

    tr = lax.broadcasted_iota(jnp.int32, (lb, lb), 0)
    tc = lax.broadcasted_iota(jnp.int32, (lb, lb), 1)
    prefix = (((tr // CHUNK) == (tc // CHUNK)) & (tc <= tr)).astype(BF16)
    hi, lo = _split(lw2)
    cum = _dot(prefix, hi) + _dot(prefix, lo)
    tot = jnp.concatenate(
        [jnp.broadcast_to(cum[c * CHUNK + CHUNK - 1:(c + 1) * CHUNK, :], (CHUNK, w))
         for c in range(lb // CHUNK)], axis=0)
    dec_out = jnp.exp2(-cum)
    dec_rest = jnp.exp2(tot - cum)
    rt_ref[...] = r * jnp.exp2(cum)
    at_ref[...] = a_vec * jnp.exp2(cum - lw2)
    bt_ref[...] = b_vec * dec_out
    kt_ref[...] = k2 * dec_out
    v_ref[...] = vr
    bh_ref[...] = b_vec * dec_rest
    kh_ref[...] = k2 * dec_rest
    wc_ref[...] = jnp.exp2(tot)

    n2 = 2 * CHUNK
    lane = lax.broadcasted_iota(jnp.int32, (CHUNK, PAIR), 1)
    first = lane < HEAD_DIM
    br = lax.broadcasted_iota(jnp.int32, (n2, n2), 0)
    bc = lax.broadcasted_iota(jnp.int32, (n2, n2), 1)
    same_head = (br // CHUNK) == (bc // CHUNK)
    strict = same_head & (bc < br)
    incl = same_head & (bc <= br)
    eye = br == bc

    def stack_masked(x):
        return jnp.concatenate([jnp.where(first, x, 0.0), jnp.where(first, 0.0, x)], axis=0)

    def stack(x):
        return jnp.concatenate([x, x], axis=0)

    per_step = 2 if (lb // CHUNK) % 2 == 0 else 1

    def chunk_step(c, carry):
        t0 = pl.multiple_of(c * (per_step * CHUNK), per_step * CHUNK)
        njob = per_step * npair
        sls = [(pl.ds(t0 + (jb // npair) * CHUNK, CHUNK),
                slice((jb % npair) * PAIR, (jb % npair + 1) * PAIR)) for jb in range(njob)]
        each = lambda f: [f(jb) for jb in range(njob)]
        a_sm = each(lambda pi: stack_masked(at_ref[sls[pi]]))
        r_sm = each(lambda pi: stack_masked(rt_ref[sls[pi]]))
        v_sm = each(lambda pi: stack_masked(v_ref[sls[pi]]).astype(BF16))
        bh_sm = each(lambda pi: stack_masked(bh_ref[sls[pi]]).astype(BF16))
        kh_sm = each(lambda pi: stack_masked(kh_ref[sls[pi]]).astype(BF16))
        wc = each(lambda pi: wc_ref[pl.ds(t0 + (pi // npair) * CHUNK, 8), sls[pi][1]][0:1, :])
        mm = each(lambda pi: _dot(
            jnp.concatenate([a_sm[pi], r_sm[pi]], axis=0).astype(BF16),
            jnp.concatenate([stack(bt_ref[sls[pi]]), stack(kt_ref[sls[pi]])], axis=0).astype(BF16),
            _NT))
        m_ab = each(lambda pi: jnp.where(strict, mm[pi][:n2, :n2], 0.0))
        m_ak = each(lambda pi: jnp.where(strict, mm[pi][:n2, n2:], 0.0).astype(BF16))
        m_rbk = each(lambda pi: jnp.concatenate(
            [jnp.where(incl, mm[pi][n2:, :n2], 0.0), jnp.where(incl, mm[pi][n2:, n2:], 0.0)],
            axis=1).astype(BF16))
        x = each(lambda pi: jnp.concatenate([a_sm[pi], _dot(m_ak[pi], v_sm[pi])], axis=1))
        lp = m_ab
        steps = CHUNK.bit_length() - 1
        for i in range(steps):
            lp16 = each(lambda pi: lp[pi].astype(BF16))
            if i < steps - 1:
                both = each(lambda pi: _dot(
                    lp16[pi], jnp.concatenate([lp16[pi], x[pi].astype(BF16)], axis=1)))
                lp = each(lambda pi: both[pi][:, :n2])
                x = each(lambda pi: x[pi] + both[pi][:, n2:])
            else:
                x = each(lambda pi: x[pi] + _dot(lp16[pi], x[pi].astype(BF16)))
        pq = each(lambda pi: x[pi].astype(BF16))
        zeros = jnp.zeros((n2, PAIR), BF16)
        gy = each(lambda pi: _dot(m_rbk[pi], jnp.concatenate(
            [pq[pi], jnp.concatenate([zeros, v_sm[pi]], axis=1)], axis=0)))
        phi = each(lambda pi: _dot(pq[pi][:, :PAIR], bh_sm[pi], _TN) + jnp.where(eye, wc[pi], 0.0))
        psi = each(lambda pi: _dot(jnp.concatenate([pq[pi][:, PAIR:], v_sm[pi]], axis=0),
                                   jnp.concatenate([bh_sm[pi], kh_sm[pi]], axis=0), _TN))
        g_pk = each(lambda pi: (r_sm[pi][:CHUNK] + gy[pi][:CHUNK, :PAIR]
                                + r_sm[pi][CHUNK:] + gy[pi][CHUNK:, :PAIR]).astype(BF16))
        y0 = each(lambda pi: gy[pi][:CHUNK, PAIR:] + gy[pi][CHUNK:, PAIR:])
        phi16 = each(lambda pi: phi[pi].astype(BF16))
        state = [state_ref[pi] for pi in range(npair)]
        ys = [None] * njob
        for ci in range(per_step):
            s16 = [st.astype(BF16) for st in state]
            for pi in range(npair):
                jb = ci * npair + pi
                ys[jb] = _dot(g_pk[jb], s16[pi], _NT) + y0[jb]
            state = [_dot(s16[pi], phi16[ci * npair + pi]) + psi[ci * npair + pi]
                     for pi in range(npair)]
        for jb in range(njob):
            y_ref[sls[jb]] = ys[jb]
        for pi in range(npair):
            state_ref[pi] = state[pi]
        return carry

    lax.fori_loop(0, lb // (per_step * CHUNK), chunk_step, 0)

    y = y_ref[...]
    inv = 1.0 / HEAD_DIM
    d = y - _head_sums(y) * inv
    var = _head_sums(d * d) * inv
    yn = d * lax.rsqrt(var + GN_EPS) * gng_ref[...] + gnb_ref[...]
    o_ref[0] = ((yn + bonus) * gate).astype(BF16)


def _rwkv(prw, mu, w0, dw2, a0, aw2, gw2, k_k, k_a, r_k, gn_g, gn_b, lb):
    b, s, n = prw.shape
    w = RWKV_WIDTH
    const = lambda bi, j: (0, 0)
    vec = lambda m: pl.BlockSpec((1, m), const)
    big = lambda: pltpu.VMEM((lb, w), F32)
    return pl.pallas_call(
        functools.partial(_rwkv_body, lb=lb),
        grid=(b, s // lb),
        in_specs=[
            pl.BlockSpec((1, lb, n), lambda bi, j: (bi, j, 0)),
            vec(n), vec(w),
            pl.BlockSpec((DECAY_LORA + AAA_LORA, w), const),
            vec(w),
            pl.BlockSpec((DECAY_LORA + AAA_LORA, w), const),
            pl.BlockSpec((GATE_LORA, w), const),
            vec(w), vec(w), vec(w), vec(w), vec(w),
        ],
        out_specs=pl.BlockSpec((1, lb, w), lambda bi, j: (bi, j, 0)),
        out_shape=jax.ShapeDtypeStruct((b, s, w), BF16),
        scratch_shapes=[
            pltpu.VMEM((1, n), F32),
            pltpu.VMEM((w // PAIR, PAIR, PAIR), F32),
            big(), big(), big(), big(), big(), big(), big(), big(), big(),
        ],
        compiler_params=pltpu.CompilerParams(
            dimension_semantics=("arbitrary", "arbitrary"), vmem_limit_bytes=VMEM_LIMIT),
        name="rwkv7",
    )(prw, mu, w0, dw2, a0, aw2, gw2, k_k, k_a, r_k, gn_g, gn_b)


def _outproj_body(sb_ref, y_ref, x_ref, wa_ref, wb_ref, g_ref, o_ref):
    m = _dot(sb_ref[...], wa_ref[...]) + _dot(y_ref[...], wb_ref[...])
    o_ref[...] = x_ref[...] + _rms(m, g_ref[...])


def _outproj(sb2d, y2d, x2d, w_out, g, tm):
    t, d = x2d.shape
    const = lambda i: (0, 0)
    return pl.pallas_call(
        _outproj_body,
        grid=(t // tm,),
        in_specs=[
            pl.BlockSpec((tm, SB_WIDTH), lambda i: (i, 0)),
            pl.BlockSpec((tm, RWKV_WIDTH), lambda i: (i, 0)),
            pl.BlockSpec((tm, d), lambda i: (i, 0)),
            pl.BlockSpec((SB_WIDTH, d), lambda i: (0, 0)),
            pl.BlockSpec((RWKV_WIDTH, d), lambda i: (1, 0)),
            pl.BlockSpec((1, d), const),
        ],
        out_specs=pl.BlockSpec((tm, d), lambda i: (i, 0)),
        out_shape=jax.ShapeDtypeStruct((t, d), F32),
        compiler_params=pltpu.CompilerParams(
            dimension_semantics=("arbitrary",), vmem_limit_bytes=VMEM_LIMIT),
        name="outproj",
    )(sb2d, y2d, x2d, w_out, w_out, g)


def _pick(n, pref):
    return pref if n % pref == 0 else n


def kernel(x, ffn1_pre_g, ffn1_post_g, ffn1_w_gate, ffn1_w_up, ffn1_w_down, mix_pre_g, mix_post_g, w_in, shift_mu, sb_out_g, decay_w0, decay_w2, iclr_a0, iclr_a2, gate_w2, k_k, k_a, r_k, gn_g, gn_b, w_out, ffn2_pre_g, ffn2_post_g, ffn2_w_gate, ffn2_w_up, ffn2_w_down):
    b, s, d = x.shape
    depth = ffn1_pre_g.shape[0]
    t = b * s
    tm = _pick(s, 512)
    blk = _pick(s, 256)
    lb = _pick(s, 256)
    x2d = x.reshape(t, d)
    for l in range(depth):
        x2d = _ffn(x2d, ffn1_pre_g[l][None], ffn1_post_g[l][None], ffn1_w_gate[l].astype(BF16),
                   ffn1_w_up[l].astype(BF16), ffn1_w_down[l].astype(BF16), tm)
        q, k, vt, prw = _inproj(x2d, mix_pre_g[l][None], w_in[l].astype(BF16), tm, s)
        sb = _sb_attention(q, k, vt, sb_out_g[l][None], blk)
        zpad = jnp.zeros((AAA_LORA, RWKV_WIDTH), F32)
        dw2 = jnp.concatenate([decay_w2[l], zpad], axis=0).astype(BF16)
        aw2 = jnp.concatenate([zpad, iclr_a2[l]], axis=0).astype(BF16)
        y = _rwkv(prw.reshape(b, s, -1), shift_mu[l][None], decay_w0[l][None], dw2,
                  iclr_a0[l][None], aw2, gate_w2[l].astype(BF16), k_k[l][None], k_a[l][None],
                  r_k[l].reshape(1, -1), gn_g[l][None], gn_b[l][None], lb)
        x2d = _outproj(sb, y.reshape(t, -1), x2d, w_out[l].astype(BF16), mix_post_g[l][None], tm)
        x2d = _ffn(x2d, ffn2_pre_g[l][None], ffn2_post_g[l][None], ffn2_w_gate[l].astype(BF16),
                   ffn2_w_up[l].astype(BF16), ffn2_w_down[l].astype(BF16), tm)
    return x2d.reshape(b, s, d)
```

```python
import functools

import jax
import jax.numpy as jnp
from jax import lax
from jax.experimental import pallas as pl
from jax.experimental.pallas import tpu as pltpu

F32 = jnp.float32
BF16 = jnp.bfloat16

HEAD_DIM = 64
SB_HEADS = 8
RWKV_HEADS = 8
SB_WIDTH = SB_HEADS * HEAD_DIM
RWKV_WIDTH = RWKV_HEADS * HEAD_DIM
DECAY_LORA = 64
AAA_LORA = 64
GATE_LORA = 128
NORM_EPS = 1e-6
GN_EPS = HEAD_DIM * 1e-5

LANES = 128
PAIR = LANES
GROUP = 2 * LANES
LOG2E = 1.4426950408889634
CHUNK = 64
VMEM_LIMIT = 56 * 1024 * 1024

_NT = (((1,), (1,)), ((), ()))
_TN = (((0,), (0,)), ((), ()))


def _dot(a, b, dims=None):
    if dims is None:
        dims = (((a.ndim - 1,), (0,)), ((), ()))
    return lax.dot_general(a, b, dims, preferred_element_type=F32)


def _rms(x, g):
    return x * lax.rsqrt(jnp.mean(x * x, axis=-1, keepdims=True) + NORM_EPS) * g


def _softplus(x):
    return jnp.maximum(x, 0.0) + jnp.log(1.0 + jnp.exp(-jnp.abs(x)))


def _sigmoid(x):
    return 1.0 / (1.0 + jnp.exp(-x))


def _split(x):
    hi = x.astype(BF16)
    return hi, (x - hi.astype(F32)).astype(BF16)


def _head_sums(x):
    r = lax.broadcasted_iota(jnp.int32, (GROUP, GROUP), 0) // HEAD_DIM
    c = lax.broadcasted_iota(jnp.int32, (GROUP, GROUP), 1) // HEAD_DIM
    ones = (r == c).astype(BF16)
    parts = [_dot(x[:, gi * GROUP:(gi + 1) * GROUP].astype(BF16), ones)
             for gi in range(x.shape[1] // GROUP)]
    return jnp.concatenate(parts, axis=1)


def _ffn_body(x_ref, pre_ref, post_ref, wg_ref, wu_ref, wd_ref, o_ref):
    x = x_ref[...]
    xn = _rms(x, pre_ref[...]).astype(BF16)
    g = _dot(xn, wg_ref[...])
    u = _dot(xn, wu_ref[...])
    h = (g * _sigmoid(g) * u).astype(BF16)
    f = _dot(h, wd_ref[...])
    o_ref[...] = x + 0.5 * _rms(f, post_ref[...])


def _ffn(x2d, pre_g, post_g, wg, wu, wd, tm):
    t, d = x2d.shape
    dff = wg.shape[1]
    const = lambda i: (0, 0)
    return pl.pallas_call(
        _ffn_body,
        grid=(t // tm,),
        in_specs=[
            pl.BlockSpec((tm, d), lambda i: (i, 0)),
            pl.BlockSpec((1, d), const),
            pl.BlockSpec((1, d), const),
            pl.BlockSpec((d, dff), const, pipeline_mode=pl.Buffered(1)),
            pl.BlockSpec((d, dff), const, pipeline_mode=pl.Buffered(1)),
            pl.BlockSpec((dff, d), const, pipeline_mode=pl.Buffered(1)),
        ],
        out_specs=pl.BlockSpec((tm, d), lambda i: (i, 0)),
        out_shape=jax.ShapeDtypeStruct((t, d), F32),
        compiler_params=pltpu.CompilerParams(
            dimension_semantics=("arbitrary",), vmem_limit_bytes=VMEM_LIMIT),
        name="ffn",
    )(x2d, pre_g, post_g, wg, wu, wd)


def _inproj_body(x_ref, g_ref, w_ref, q_ref, k_ref, vt_ref, prw_ref):
    xn = _rms(x_ref[...], g_ref[...]).astype(BF16)
    p = _dot(xn, w_ref[...])
    q_ref[...] = (p[:, :SB_WIDTH] * (HEAD_DIM ** -0.5 * LOG2E)).astype(BF16)
    k_ref[...] = p[:, SB_WIDTH:2 * SB_WIDTH].astype(BF16)
    vt_ref[0] = p[:, 2 * SB_WIDTH:3 * SB_WIDTH].T.astype(BF16)
    prw_ref[...] = p[:, 3 * SB_WIDTH:]


def _inproj(x2d, g, w_in, tm, seq):
    t, d = x2d.shape
    n = w_in.shape[1]
    nrw = n - 3 * SB_WIDTH
    per_seq = seq // tm
    const = lambda i: (0, 0)
    row = lambda i: (i, 0)
    return pl.pallas_call(
        _inproj_body,
        grid=(t // tm,),
        in_specs=[
            pl.BlockSpec((tm, d), row),
            pl.BlockSpec((1, d), const),
            pl.BlockSpec((d, n), const, pipeline_mode=pl.Buffered(1)),
        ],
        out_specs=[
            pl.BlockSpec((tm, SB_WIDTH), row),
            pl.BlockSpec((tm, SB_WIDTH), row),
            pl.BlockSpec((1, SB_WIDTH, tm), lambda i: (i // per_seq, 0, i % per_seq)),
            pl.BlockSpec((tm, nrw), row),
        ],
        out_shape=[
            jax.ShapeDtypeStruct((t, SB_WIDTH), BF16),
            jax.ShapeDtypeStruct((t, SB_WIDTH), BF16),
            jax.ShapeDtypeStruct((t // seq, SB_WIDTH, seq), BF16),
            jax.ShapeDtypeStruct((t, nrw), F32),
        ],
        compiler_params=pltpu.CompilerParams(
            dimension_semantics=("arbitrary",), vmem_limit_bytes=VMEM_LIMIT),
        name="inproj",
    )(x2d, g, w_in)


def _sb_body(q_ref, k_ref, vt_ref, g_ref, o_ref, acc_ref, qh_ref, *stage_refs, blk):
    qi = pl.program_id(2)
    nh = GROUP // HEAD_DIM
    halves = tuple((h,) for h in range(nh))
    names = ("zraw", "craw", "z", "sp", "tot")
    bufs = tuple(dict(zip(names, stage_refs[len(names) * h:len(names) * (h + 1)]))
                 for h in range(nh))

    q = q_ref[...]
    lane = lax.broadcasted_iota(jnp.int32, (blk, GROUP), 1)
    zero = jnp.zeros_like(q)
    for h in range(nh):
        qh_ref[h] = jnp.where(lane // HEAD_DIM == h, q, zero)
    key = lax.broadcasted_iota(jnp.int32, (blk, blk), 0)
    qry = lax.broadcasted_iota(jnp.int32, (blk, blk), 1)
    neg_after = jnp.where(qry > key, -1.0, 0.0).astype(BF16)
    causal = key < qry

    def rows_of(kb):
        return pl.ds(pl.multiple_of(kb * blk, blk), blk)

    def part(ew=None, mx=None):
        if mx is not None:
            mhalf, kb_s, do_sums = mx
            mb = bufs[mhalf]
            if do_sums:
                sums = [_dot(neg_after, mb["sp"][i]) for i in range(len(halves[mhalf]))]
            if kb_s is not None:
                kblk = k_ref[rows_of(kb_s), :]
                raw = [_dot(kblk, qh_ref[h], _NT) for h in halves[mhalf]]
        new = None
        if ew is not None:
            ehalf, kb_w, do_score, diagonal, carries = ew
            eb = bufs[ehalf]
            new = list(carries)
            outs = []
            if kb_w is not None:
                for i, h in enumerate(halves[ehalf]):
                    new[h] = eb["tot"][i] + carries[h]
                    w = jnp.exp2(eb["z"][i] + eb["craw"][i])
                    vt = vt_ref[0, h * HEAD_DIM:(h + 1) * HEAD_DIM, rows_of(kb_w)]
                    outs.append(_dot(vt, w) * jnp.exp2(carries[h]))
            if do_score:
                for i in range(len(halves[ehalf])):
                    z = eb["zraw"][i]
                    if diagonal:
                        z = jnp.where(causal, z, -1e30)
                    m = jnp.maximum(z, 0.0)
                    soft = jnp.log(1.0 + jnp.exp2(-jnp.abs(z))) * LOG2E
                    eb["z"][i] = (z - m) - soft
                    eb["sp"][i] = m + soft
        if mx is not None:
            for i in range(len(halves[mhalf])):
                if do_sums:
                    mb["tot"][i] = sums[i][0:1, :] - mb["sp"][i][0:1, :].astype(F32)
                    mb["craw"][i] = sums[i].astype(BF16)
                if kb_s is not None:
                    mb["zraw"][i] = raw[i].astype(BF16)
        if ew is not None and kb_w is not None:
            for i, h in enumerate(halves[ehalf]):
                acc_ref[h] += outs[i]
        return None if new is None else tuple(new)

    acc_ref[...] = jnp.zeros_like(acc_ref)
    carries = (jnp.zeros((1, blk), F32),) * nh
    lead = nh // 2

    def one_step(kb_w, kb_s, diagonal, carries, last=False):
        for h in range(nh):
            ahead = h + lead
            if ahead < nh:
                mx = (ahead, kb_s, kb_w is not None)
            elif last:
                mx = None
            else:
                mx = (ahead - nh, jnp.maximum(kb_s - 1, 0), True)
            carries = part(ew=(h, kb_w, kb_s is not None, diagonal, carries), mx=mx)
        return carries

    for h in range(lead):
        part(mx=(h, qi, False))
    carries = one_step(None, qi, True, carries)
    carries = lax.fori_loop(
        1, qi + 1, lambda s, cs: one_step(qi - s + 1, qi - s, False, cs), carries)
    one_step(0, None, False, carries, last=True)

    parts = []
    for h in range(nh):
        a = acc_ref[h]
        ms = jnp.mean(a * a, axis=0, keepdims=True)
        parts.append(a * lax.rsqrt(ms + NORM_EPS))
    out = jnp.concatenate(parts, axis=0).T
    o_ref[...] = (out * g_ref[...]).astype(BF16)


def _sb_attention(q, k, vt, sb_g, blk):
    t = q.shape[0]
    b, _, s = vt.shape
    ngroup = SB_WIDTH // GROUP
    nq = s // blk
    nh = GROUP // HEAD_DIM
    tile = pltpu.VMEM((1, blk, blk), BF16)
    row = pltpu.VMEM((1, 1, blk), F32)
    return pl.pallas_call(
        functools.partial(_sb_body, blk=blk),
        grid=(b, ngroup, nq),
        in_specs=[
            pl.BlockSpec((blk, GROUP), lambda bi, g, qi: (bi * nq + qi, g)),
            pl.BlockSpec((s, GROUP), lambda bi, g, qi: (bi, g)),
            pl.BlockSpec((1, GROUP, s), lambda bi, g, qi: (bi, g, 0)),
            pl.BlockSpec((1, GROUP), lambda bi, g, qi: (0, g)),
        ],
        out_specs=pl.BlockSpec((blk, GROUP), lambda bi, g, qi: (bi * nq + qi, g)),
        out_shape=jax.ShapeDtypeStruct((t, SB_WIDTH), BF16),
        scratch_shapes=[
            pltpu.VMEM((nh, HEAD_DIM, blk), F32),
            pltpu.VMEM((nh, blk, GROUP), BF16),
        ] + [tile, tile, tile, tile, row] * nh,
        compiler_params=pltpu.CompilerParams(
            dimension_semantics=("arbitrary", "arbitrary", "arbitrary"),
            vmem_limit_bytes=VMEM_LIMIT),
        name="sb_attention",
    )(q, k, vt, sb_g)


def _rwkv_body(p_ref, mu_ref, w0_ref, dw2_ref, a0_ref, aw2_ref, gw2_ref, kk_ref, ka_ref,
               rk_ref, gng_ref, gnb_ref, o_ref,
               last_ref, state_ref, rt_ref, at_ref, bt_ref, kt_ref, v_ref, bh_ref, kh_ref,
               wc_ref, y_ref, *, lb):
    j = pl.program_id(1)
    w = RWKV_WIDTH
    npair = w // PAIR

    @pl.when(j == 0)
    def _():
        last_ref[...] = jnp.zeros_like(last_ref)
        state_ref[...] = jnp.zeros_like(state_ref)

    p = p_ref[0]
    rows = lax.broadcasted_iota(jnp.int32, p.shape, 0)
    prev = jnp.where(rows == 0, last_ref[...], pltpu.roll(p, 1, 0))
    last_ref[...] = p[lb - 1:lb, :]
    xs = p + (prev - p) * mu_ref[...]
    r = xs[:, 0:w]
    kr = xs[:, w:2 * w]
    vr = xs[:, 2 * w:3 * w]
    lora_in = xs[:, 3 * w:3 * w + DECAY_LORA + AAA_LORA]
    gd = xs[:, 3 * w + DECAY_LORA + AAA_LORA:]
    lora_w = _dot(jnp.tanh(lora_in).astype(BF16), dw2_ref[...])
    lora_a = _dot(lora_in.astype(BF16), aw2_ref[...])
    gate = _dot(_sigmoid(gd).astype(BF16), gw2_ref[...])
    w_log = -_softplus(-(w0_ref[...] + lora_w)) - 0.5
    lw2 = jnp.exp(w_log) * (-LOG2E)
    a = _sigmoid(a0_ref[...] + lora_a)
    kk = kr * kk_ref[...]
    k2 = kr * (1.0 + (a - 1.0) * ka_ref[...])
    bonus = _head_sums(r * k2 * rk_ref[...]) * vr
    kkn = kk * lax.rsqrt(jnp.maximum(_head_sums(kk * kk), 1e-24))
    a_vec = -kkn
    b_vec = kkn * a

    tr = lax.broadcasted_iota(jnp.int32, (lb, lb), 0)
    tc = lax.broadcasted_iota(jnp.int32, (lb, lb), 1)
    prefix = (((tr // CHUNK) == (tc // CHUNK)) & (tc <= tr)).astype(BF16)
    hi, lo = _split(lw2)
    cum = _dot(prefix, hi) + _dot(prefix, lo)
    tot = jnp.concatenate(
        [jnp.broadcast_to(cum[c * CHUNK + CHUNK - 1:(c + 1) * CHUNK, :], (CHUNK, w))
         for c in range(lb // CHUNK)], axis=0)
    dec_out = jnp.exp2(-cum)
    dec_rest = jnp.exp2(tot - cum)
    rt_ref[...] = r * jnp.exp2(cum)
    at_ref[...] = a_vec * jnp.exp2(cum - lw2)
    bt_ref[...] = b_vec * dec_out
    kt_ref[...] = k2 * dec_out
    v_ref[...] = vr
    bh_ref[...] = b_vec * dec_rest
    kh_ref[...] = k2 * dec_rest
    wc_ref[...] = jnp.exp2(tot)

    n2 = 2 * CHUNK
    lane = lax.broadcasted_iota(jnp.int32, (CHUNK, PAIR), 1)
    first = lane < HEAD_DIM
    br = lax.broadcasted_iota(jnp.int32, (n2, n2), 0)
    bc = lax.broadcasted_iota(jnp.int32, (n2, n2), 1)
    same_head = (br // CHUNK) == (bc // CHUNK)
    strict = same_head & (bc < br)
    incl = same_head & (bc <= br)
    eye = br == bc

    def stack_masked(x):
        return jnp.concatenate([jnp.where(first, x, 0.0), jnp.where(first, 0.0, x)], axis=0)

    def stack(x):
        return jnp.concatenate([x, x], axis=0)

    per_step = 2 if (lb // CHUNK) % 2 == 0 else 1

    def chunk_step(c, carry):
        t0 = pl.multiple_of(c * (per_step * CHUNK), per_step * CHUNK)
        njob = per_step * npair
        sls = [(pl.ds(t0 + (jb // npair) * CHUNK, CHUNK),
                slice((jb % npair) * PAIR, (jb % npair + 1) * PAIR)) for jb in range(njob)]
        each = lambda f: [f(jb) for jb in range(njob)]
        a_sm = each(lambda pi: stack_masked(at_ref[sls[pi]]))
        r_sm = each(lambda pi: stack_masked(rt_ref[sls[pi]]))
        v_sm = each(lambda pi: stack_masked(v_ref[sls[pi]]).astype(BF16))
        bh_sm = each(lambda pi: stack_masked(bh_ref[sls[pi]]).astype(BF16))
        kh_sm = each(lambda pi: stack_masked(kh_ref[sls[pi]]).astype(BF16))
        wc = each(lambda pi: wc_ref[pl.ds(t0 + (pi // npair) * CHUNK, 8), sls[pi][1]][0:1, :])
        mm = each(lambda pi: _dot(
            jnp.concatenate([a_sm[pi], r_sm[pi]], axis=0).astype(BF16),
            jnp.concatenate([stack(bt_ref[sls[pi]]), stack(kt_ref[sls[pi]])], axis=0).astype(BF16),
            _NT))
        m_ab = each(lambda pi: jnp.where(strict, mm[pi][:n2, :n2], 0.0))
        m_ak = each(lambda pi: jnp.where(strict, mm[pi][:n2, n2:], 0.0).astype(BF16))
        m_rbk = each(lambda pi: jnp.concatenate(
            [jnp.where(incl, mm[pi][n2:, :n2], 0.0), jnp.where(incl, mm[pi][n2:, n2:], 0.0)],
            axis=1).astype(BF16))
        x = each(lambda pi: jnp.concatenate([a_sm[pi], _dot(m_ak[pi], v_sm[pi])], axis=1))
        lp = m_ab
        steps = CHUNK.bit_length() - 1
        for i in range(steps):
            lp16 = each(lambda pi: lp[pi].astype(BF16))
            if i < steps - 1:
                both = each(lambda pi: _dot(
                    lp16[pi], jnp.concatenate([lp16[pi], x[pi].astype(BF16)], axis=1)))
                lp = each(lambda pi: both[pi][:, :n2])
                x = each(lambda pi: x[pi] + both[pi][:, n2:])
            else:
                x = each(lambda pi: x[pi] + _dot(lp16[pi], x[pi].astype(BF16)))
        pq = each(lambda pi: x[pi].astype(BF16))
        zeros = jnp.zeros((n2, PAIR), BF16)
        gy = each(lambda pi: _dot(m_rbk[pi], jnp.concatenate(
            [pq[pi], jnp.concatenate([zeros, v_sm[pi]], axis=1)], axis=0)))
        phi = each(lambda pi: _dot(pq[pi][:, :PAIR], bh_sm[pi], _TN) + jnp.where(eye, wc[pi], 0.0))
        psi = each(lambda pi: _dot(jnp.concatenate([pq[pi][:, PAIR:], v_sm[pi]], axis=0),
                                   jnp.concatenate([bh_sm[pi], kh_sm[pi]], axis=0), _TN))
        g_pk = each(lambda pi: (r_sm[pi][:CHUNK] + gy[pi][:CHUNK, :PAIR]
                                + r_sm[pi][CHUNK:] + gy[pi][CHUNK:, :PAIR]).astype(BF16))
        y0 = each(lambda pi: gy[pi][:CHUNK, PAIR:] + gy[pi][CHUNK:, PAIR:])
        phi16 = each(lambda pi: phi[pi].astype(BF16))
        state = [state_ref[pi] for pi in range(npair)]
        ys = [None] * njob
        for ci in range(per_step):
            s16 = [st.astype(BF16) for st in state]
            for pi in range(npair):
                jb = ci * npair + pi
                ys[jb] = _dot(g_pk[jb], s16[pi], _NT) + y0[jb]
            state = [_dot(s16[pi], phi16[ci * npair + pi]) + psi[ci * npair + pi]
                     for pi in range(npair)]
        for jb in range(njob):
            y_ref[sls[jb]] = ys[jb]
        for pi in range(npair):
            state_ref[pi] = state[pi]
        return carry

    lax.fori_loop(0, lb // (per_step * CHUNK), chunk_step, 0)

    y = y_ref[...]
    inv = 1.0 / HEAD_DIM
    d = y - _head_sums(y) * inv
    var = _head_sums(d * d) * inv
    yn = d * lax.rsqrt(var + GN_EPS) * gng_ref[...] + gnb_ref[...]
    o_ref[0] = ((yn + bonus) * gate).astype(BF16)


def _rwkv(prw, mu, w0, dw2, a0, aw2, gw2, k_k, k_a, r_k, gn_g, gn_b, lb):
    b, s, n = prw.shape
    w = RWKV_WIDTH
    const = lambda bi, j: (0, 0)
    vec = lambda m: pl.BlockSpec((1, m), const)
    big = lambda: pltpu.VMEM((lb, w), F32)
    return pl.pallas_call(
        functools.partial(_rwkv_body, lb=lb),
        grid=(b, s // lb),
        in_specs=[
            pl.BlockSpec((1, lb, n), lambda bi, j: (bi, j, 0)),
            vec(n), vec(w),
            pl.BlockSpec((DECAY_LORA + AAA_LORA, w), const),
            vec(w),
            pl.BlockSpec((DECAY_LORA + AAA_LORA, w), const),
            pl.BlockSpec((GATE_LORA, w), const),
            vec(w), vec(w), vec(w), vec(w), vec(w),
        ],
        out_specs=pl.BlockSpec((1, lb, w), lambda bi, j: (bi, j, 0)),
        out_shape=jax.ShapeDtypeStruct((b, s, w), BF16),
        scratch_shapes=[
            pltpu.VMEM((1, n), F32),
            pltpu.VMEM((w // PAIR, PAIR, PAIR), F32),
            big(), big(), big(), big(), big(), big(), big(), big(), big(),
        ],
        compiler_params=pltpu.CompilerParams(
            dimension_semantics=("arbitrary", "arbitrary"), vmem_limit_bytes=VMEM_LIMIT),
        name="rwkv7",
    )(prw, mu, w0, dw2, a0, aw2, gw2, k_k, k_a, r_k, gn_g, gn_b)


def _outproj_ffn_body(sb_ref, y_ref, x_ref, wa_ref, wb_ref, mg_ref, pre_ref, post_ref,
                      wg_ref, wu_ref, wd_ref, o_ref):
    m = _dot(sb_ref[...], wa_ref[...]) + _dot(y_ref[...], wb_ref[...])
    x = x_ref[...] + _rms(m, mg_ref[...])
    xn = _rms(x, pre_ref[...]).astype(BF16)
    g = _dot(xn, wg_ref[...])
    u = _dot(xn, wu_ref[...])
    h = (g * _sigmoid(g) * u).astype(BF16)
    f = _dot(h, wd_ref[...])
    o_ref[...] = x + 0.5 * _rms(f, post_ref[...])


def _outproj_ffn(sb2d, y2d, x2d, w_out, mix_g, pre_g, post_g, wg, wu, wd, tm):
    t, d = x2d.shape
    dff = wg.shape[1]
    const = lambda i: (0, 0)
    row = lambda i: (i, 0)
    once = dict(pipeline_mode=pl.Buffered(1))
    return pl.pallas_call(
        _outproj_ffn_body,
        grid=(t // tm,),
        in_specs=[
            pl.BlockSpec((tm, SB_WIDTH), row),
            pl.BlockSpec((tm, RWKV_WIDTH), row),
            pl.BlockSpec((tm, d), row),
            pl.BlockSpec((SB_WIDTH, d), lambda i: (0, 0), **once),
            pl.BlockSpec((RWKV_WIDTH, d), lambda i: (1, 0), **once),
            pl.BlockSpec((1, d), const),
            pl.BlockSpec((1, d), const),
            pl.BlockSpec((1, d), const),
            pl.BlockSpec((d, dff), const, **once),
            pl.BlockSpec((d, dff), const, **once),
            pl.BlockSpec((dff, d), const, **once),
        ],
        out_specs=pl.BlockSpec((tm, d), row),
        out_shape=jax.ShapeDtypeStruct((t, d), F32),
        compiler_params=pltpu.CompilerParams(
            dimension_semantics=("arbitrary",), vmem_limit_bytes=VMEM_LIMIT),
        name="outproj_ffn",
    )(sb2d, y2d, x2d, w_out, w_out, mix_g, pre_g, post_g, wg, wu, wd)


def _pick(n, pref):
    return pref if n % pref == 0 else n


def kernel(x, ffn1_pre_g, ffn1_post_g, ffn1_w_gate, ffn1_w_up, ffn1_w_down, mix_pre_g, mix_post_g, w_in, shift_mu, sb_out_g, decay_w0, decay_w2, iclr_a0, iclr_a2, gate_w2, k_k, k_a, r_k, gn_g, gn_b, w_out, ffn2_pre_g, ffn2_post_g, ffn2_w_gate, ffn2_w_up, ffn2_w_down):
    b, s, d = x.shape
    depth = ffn1_pre_g.shape[0]
    t = b * s
    tm = _pick(s, 512)
    blk = _pick(s, 256)
    lb = _pick(s, 256)
    x2d = x.reshape(t, d)
    for l in range(depth):
        x2d = _ffn(x2d, ffn1_pre_g[l][None], ffn1_post_g[l][None], ffn1_w_gate[l].astype(BF16),
                   ffn1_w_up[l].astype(BF16), ffn1_w_down[l].astype(BF16), tm)
        q, k, vt, prw = _inproj(x2d, mix_pre_g[l][None], w_in[l].astype(BF16), tm, s)
        sb = _sb_attention(q, k, vt, sb_out_g[l][None], blk)
        zpad = jnp.zeros((AAA_LORA, RWKV_WIDTH), F32)
        dw2 = jnp.concatenate([decay_w2[l], zpad], axis=0).astype(BF16)
        aw2 = jnp.concatenate([zpad, iclr_a2[l]], axis=0).astype(BF16)
        y = _rwkv(prw.reshape(b, s, -1), shift_mu[l][None], decay_w0[l][None], dw2,
                  iclr_a0[l][None], aw2, gate_w2[l].astype(BF16), k_k[l][None], k_a[l][None],
                  r_k[l].reshape(1, -1), gn_g[l][None], gn_b[l][None], lb)
        x2d = _outproj_ffn(sb, y.reshape(t, -1), x2d, w_out[l].astype(BF16), mix_post_g[l][None],
                           ffn2_pre_g[l][None], ffn2_post_g[l][None], ffn2_w_gate[l].astype(BF16),
                           ffn2_w_up[l].astype(BF16), ffn2_w_down[l].astype(BF16), tm)
    return x2d.reshape(b, s, d)
```

```python
import functools

import jax
import jax.numpy as jnp
from jax import lax
from jax.experimental import pallas as pl
from jax.experimental.pallas import tpu as pltpu

F32 = jnp.float32
BF16 = jnp.bfloat16

HEAD_DIM = 64
SB_HEADS = 8
RWKV_HEADS = 8
SB_WIDTH = SB_HEADS * HEAD_DIM
RWKV_WIDTH = RWKV_HEADS * HEAD_DIM
DECAY_LORA = 64
AAA_LORA = 64
GATE_LORA = 128
NORM_EPS = 1e-6
GN_EPS = HEAD_DIM * 1e-5

LANES = 128
PAIR = LANES
GROUP = 2 * LANES
LOG2E = 1.4426950408889634
CHUNK = 64
VMEM_LIMIT = 56 * 1024 * 1024

_NT = (((1,), (1,)), ((), ()))
_TN = (((0,), (0,)), ((), ()))


def _dot(a, b, dims=None):
    if dims is None:
        dims = (((a.ndim - 1,), (0,)), ((), ()))
    return lax.dot_general(a, b, dims, preferred_element_type=F32)


def _rms(x, g):
    return x * lax.rsqrt(jnp.mean(x * x, axis=-1, keepdims=True) + NORM_EPS) * g


def _softplus(x):
    return jnp.maximum(x, 0.0) + jnp.log(1.0 + jnp.exp(-jnp.abs(x)))


def _sigmoid(x):
    return 1.0 / (1.0 + jnp.exp(-x))


def _split(x):
    hi = x.astype(BF16)
    return hi, (x - hi.astype(F32)).astype(BF16)


def _head_sums(x):
    r = lax.broadcasted_iota(jnp.int32, (GROUP, GROUP), 0) // HEAD_DIM
    c = lax.broadcasted_iota(jnp.int32, (GROUP, GROUP), 1) // HEAD_DIM
    ones = (r == c).astype(BF16)
    parts = [_dot(x[:, gi * GROUP:(gi + 1) * GROUP].astype(BF16), ones)
             for gi in range(x.shape[1] // GROUP)]
    return jnp.concatenate(parts, axis=1)


def _ffn_body(x_ref, pre_ref, post_ref, wg_ref, wu_ref, wd_ref, o_ref):
    x = x_ref[...]
    xn = _rms(x, pre_ref[...]).astype(BF16)
    g = _dot(xn, wg_ref[...])
    u = _dot(xn, wu_ref[...])
    h = (g * _sigmoid(g) * u).astype(BF16)
    f = _dot(h, wd_ref[...])
    o_ref[...] = x + 0.5 * _rms(f, post_ref[...])


def _ffn(x2d, pre_g, post_g, wg, wu, wd, tm):
    t, d = x2d.shape
    dff = wg.shape[1]
    const = lambda i: (0, 0)
    return pl.pallas_call(
        _ffn_body,
        grid=(t // tm,),
        in_specs=[
            pl.BlockSpec((tm, d), lambda i: (i, 0)),
            pl.BlockSpec((1, d), const),
            pl.BlockSpec((1, d), const),
            pl.BlockSpec((d, dff), const, pipeline_mode=pl.Buffered(1)),
            pl.BlockSpec((d, dff), const, pipeline_mode=pl.Buffered(1)),
            pl.BlockSpec((dff, d), const, pipeline_mode=pl.Buffered(1)),
        ],
        out_specs=pl.BlockSpec((tm, d), lambda i: (i, 0)),
        out_shape=jax.ShapeDtypeStruct((t, d), F32),
        compiler_params=pltpu.CompilerParams(
            dimension_semantics=("arbitrary",), vmem_limit_bytes=VMEM_LIMIT),
        name="ffn",
    )(x2d, pre_g, post_g, wg, wu, wd)


def _inproj_body(x_ref, g_ref, w_ref, mu_ref, w0_ref, dw2_ref, a0_ref, aw2_ref, gw2_ref,
                 kk_ref, ka_ref, rk_ref,
                 q_ref, k_ref, vt_ref, rt_ref, at_ref, bt_ref, kt_ref, v_ref, bh_ref, kh_ref,
                 wc_ref, bonus_ref, gate_ref, last_ref, *, per_seq):
    i = pl.program_id(0)
    w = RWKV_WIDTH
    nsb = 3 * SB_WIDTH
    tm = x_ref.shape[0]

    @pl.when(i % per_seq == 0)
    def _():
        last_ref[...] = jnp.zeros_like(last_ref)

    xn = _rms(x_ref[...], g_ref[...]).astype(BF16)
    p = _dot(xn, w_ref[:, nsb:])
    psb = _dot(xn, w_ref[:, :nsb])
    q_ref[...] = (psb[:, :SB_WIDTH] * (HEAD_DIM ** -0.5 * LOG2E)).astype(BF16)
    k_ref[...] = psb[:, SB_WIDTH:2 * SB_WIDTH].astype(BF16)
    vt_ref[0] = psb[:, 2 * SB_WIDTH:].T.astype(BF16)

    rows = lax.broadcasted_iota(jnp.int32, p.shape, 0)
    prev = jnp.where(rows == 0, last_ref[...], pltpu.roll(p, 1, 0))
    last_ref[...] = p[tm - 1:tm, :]
    xs = p + (prev - p) * mu_ref[...]
    r = xs[:, 0:w]
    kr = xs[:, w:2 * w]
    vr = xs[:, 2 * w:3 * w]
    lora_in = xs[:, 3 * w:3 * w + DECAY_LORA + AAA_LORA]
    gd = xs[:, 3 * w + DECAY_LORA + AAA_LORA:]
    lora_w = _dot(jnp.tanh(lora_in).astype(BF16), dw2_ref[...])
    lora_a = _dot(lora_in.astype(BF16), aw2_ref[...])
    gate_ref[...] = _dot(_sigmoid(gd).astype(BF16), gw2_ref[...]).astype(BF16)
    w_log = -_softplus(-(w0_ref[...] + lora_w)) - 0.5
    lw2 = jnp.exp(w_log) * (-LOG2E)
    a = _sigmoid(a0_ref[...] + lora_a)
    kk = kr * kk_ref[...]
    k2 = kr * (1.0 + (a - 1.0) * ka_ref[...])
    bonus_ref[...] = (_head_sums(r * k2 * rk_ref[...]) * vr).astype(BF16)
    kkn = kk * lax.rsqrt(jnp.maximum(_head_sums(kk * kk), 1e-24))
    a_vec = -kkn
    b_vec = kkn * a
    v_ref[...] = vr.astype(BF16)

    sub = 4 * CHUNK
    tr = lax.broadcasted_iota(jnp.int32, (sub, sub), 0)
    tc = lax.broadcasted_iota(jnp.int32, (sub, sub), 1)
    prefix = (((tr // CHUNK) == (tc // CHUNK)) & (tc <= tr)).astype(BF16)
    hi, lo = _split(lw2)
    cums, tots, ends = [], [], []
    for s0 in range(0, tm, sub):
        cum = _dot(prefix, hi[s0:s0 + sub]) + _dot(prefix, lo[s0:s0 + sub])
        cums.append(cum)
        for c in range(sub // CHUNK):
            end = cum[c * CHUNK + CHUNK - 1:(c + 1) * CHUNK, :]
            ends.append(jnp.broadcast_to(end, (8, w)))
            tots.append(jnp.broadcast_to(end, (CHUNK, w)))
    cum = jnp.concatenate(cums, axis=0)
    tot = jnp.concatenate(tots, axis=0)
    dec_out = jnp.exp2(-cum)
    dec_rest = jnp.exp2(tot - cum)
    rt_ref[...] = (r * jnp.exp2(cum)).astype(BF16)
    at_ref[...] = (a_vec * jnp.exp2(cum - lw2)).astype(BF16)
    bt_ref[...] = (b_vec * dec_out).astype(BF16)
    kt_ref[...] = (k2 * dec_out).astype(BF16)
    bh_ref[...] = (b_vec * dec_rest).astype(BF16)
    kh_ref[...] = (k2 * dec_rest).astype(BF16)
    wc_ref[...] = jnp.exp2(jnp.concatenate(ends, axis=0))


def _inproj(x2d, g, w_in, mu, w0, dw2, a0, aw2, gw2, k_k, k_a, r_k, tm, seq):
    t, d = x2d.shape
    n = w_in.shape[1]
    w = RWKV_WIDTH
    per_seq = seq // tm
    const = lambda i: (0, 0)
    row = lambda i: (i, 0)
    vec = lambda m: pl.BlockSpec((1, m), const)
    tile = pl.BlockSpec((tm, w), row)
    wide = jax.ShapeDtypeStruct((t, w), BF16)
    return pl.pallas_call(
        functools.partial(_inproj_body, per_seq=per_seq),
        grid=(t // tm,),
        in_specs=[
            pl.BlockSpec((tm, d), row),
            vec(d),
            pl.BlockSpec((d, n), const, pipeline_mode=pl.Buffered(1)),
            vec(n - 3 * SB_WIDTH), vec(w),
            pl.BlockSpec((DECAY_LORA + AAA_LORA, w), const),
            vec(w),
            pl.BlockSpec((DECAY_LORA + AAA_LORA, w), const),
            pl.BlockSpec((GATE_LORA, w), const),
            vec(w), vec(w), vec(w),
        ],
        out_specs=[
            pl.BlockSpec((tm, SB_WIDTH), row),
            pl.BlockSpec((tm, SB_WIDTH), row),
            pl.BlockSpec((1, SB_WIDTH, tm), lambda i: (i // per_seq, 0, i % per_seq)),
            tile, tile, tile, tile, tile, tile, tile,
            pl.BlockSpec((tm // 8, w), row),
            tile, tile,
        ],
        out_shape=[
            jax.ShapeDtypeStruct((t, SB_WIDTH), BF16),
            jax.ShapeDtypeStruct((t, SB_WIDTH), BF16),
            jax.ShapeDtypeStruct((t // seq, SB_WIDTH, seq), BF16),
            wide, wide, wide, wide, wide, wide, wide,
            jax.ShapeDtypeStruct((t // 8, w), F32),
            wide, wide,
        ],
        scratch_shapes=[pltpu.VMEM((1, n - 3 * SB_WIDTH), F32)],
        compiler_params=pltpu.CompilerParams(
            dimension_semantics=("arbitrary",), vmem_limit_bytes=VMEM_LIMIT),
        name="inproj",
    )(x2d, g, w_in, mu, w0, dw2, a0, aw2, gw2, k_k, k_a, r_k)


def _sb_body(q_ref, k_ref, vt_ref, g_ref, o_ref, acc_ref, qh_ref, *stage_refs, blk):
    qi = pl.program_id(2)
    nh = GROUP // HEAD_DIM
    halves = tuple((h,) for h in range(nh))
    names = ("zraw", "craw", "z", "sp", "tot")
    bufs = tuple(dict(zip(names, stage_refs[len(names) * h:len(names) * (h + 1)]))
                 for h in range(nh))

    q = q_ref[...]
    lane = lax.broadcasted_iota(jnp.int32, (blk, GROUP), 1)
    zero = jnp.zeros_like(q)
    for h in range(nh):
        qh_ref[h] = jnp.where(lane // HEAD_DIM == h, q, zero)
    key = lax.broadcasted_iota(jnp.int32, (blk, blk), 0)
    qry = lax.broadcasted_iota(jnp.int32, (blk, blk), 1)
    neg_after = jnp.where(qry > key, -1.0, 0.0).astype(BF16)
    causal = key < qry

    def rows_of(kb):
        return pl.ds(pl.multiple_of(kb * blk, blk), blk)

    def part(ew=None, mx=None):
        if mx is not None:
            mhalf, kb_s, do_sums = mx
            mb = bufs[mhalf]
            if do_sums:
                sums = [_dot(neg_after, mb["sp"][i]) for i in range(len(halves[mhalf]))]
            if kb_s is not None:
                kblk = k_ref[rows_of(kb_s), :]
                raw = [_dot(kblk, qh_ref[h], _NT) for h in halves[mhalf]]
        new = None
        if ew is not None:
            ehalf, kb_w, do_score, diagonal, carries = ew
            eb = bufs[ehalf]
            new = list(carries)
            outs = []
            if kb_w is not None:
                for i, h in enumerate(halves[ehalf]):
                    new[h] = eb["tot"][i] + carries[h]
                    w = jnp.exp2(eb["z"][i] + eb["craw"][i])
                    vt = vt_ref[0, h * HEAD_DIM:(h + 1) * HEAD_DIM, rows_of(kb_w)]
                    outs.append(_dot(vt, w) * jnp.exp2(carries[h]))
            if do_score:
                for i in range(len(halves[ehalf])):
                    z = eb["zraw"][i]
                    if diagonal:
                        z = jnp.where(causal, z, -1e30)
                    m = jnp.maximum(z, 0.0)
                    e = jnp.exp2(-jnp.abs(z)).astype(F32)
                    soft = (jnp.log(1.0 + e) * LOG2E).astype(BF16)
                    eb["z"][i] = (z - m) - soft
                    eb["sp"][i] = m + soft
        if mx is not None:
            for i in range(len(halves[mhalf])):
                if do_sums:
                    mb["tot"][i] = sums[i][0:1, :] - mb["sp"][i][0:1, :].astype(F32)
                    mb["craw"][i] = sums[i].astype(BF16)
                if kb_s is not None:
                    mb["zraw"][i] = raw[i].astype(BF16)
        if ew is not None and kb_w is not None:
            for i, h in enumerate(halves[ehalf]):
                acc_ref[h] += outs[i]
        return None if new is None else tuple(new)

    acc_ref[...] = jnp.zeros_like(acc_ref)
    carries = (jnp.zeros((1, blk), F32),) * nh
    lead = nh // 2

    def one_step(kb_w, kb_s, diagonal, carries, last=False):
        for h in range(nh):
            ahead = h + lead
            if ahead < nh:
                mx = (ahead, kb_s, kb_w is not None)
            elif last:
                mx = None
            else:
                mx = (ahead - nh, jnp.maximum(kb_s - 1, 0), True)
            carries = part(ew=(h, kb_w, kb_s is not None, diagonal, carries), mx=mx)
        return carries

    for h in range(lead):
        part(mx=(h, qi, False))
    carries = one_step(None, qi, True, carries)
    odd = qi % 2
    carries = lax.cond(odd == 1, lambda cs: one_step(qi, qi - 1, False, cs), lambda cs: cs,
                       carries)

    def two_steps(j, cs):
        kb_w = qi - odd - 2 * j
        cs = one_step(kb_w, kb_w - 1, False, cs)
        return one_step(kb_w - 1, kb_w - 2, False, cs)

    carries = lax.fori_loop(0, qi // 2, two_steps, carries)
    one_step(0, None, False, carries, last=True)

    parts = []
    for h in range(nh):
        a = acc_ref[h]
        ms = jnp.mean(a * a, axis=0, keepdims=True)
        parts.append(a * lax.rsqrt(ms + NORM_EPS))
    out = jnp.concatenate(parts, axis=0).T
    o_ref[...] = (out * g_ref[...]).astype(BF16)


def _sb_attention(q, k, vt, sb_g, blk):
    t = q.shape[0]
    b, _, s = vt.shape
    ngroup = SB_WIDTH // GROUP
    nq = s // blk
    nh = GROUP // HEAD_DIM
    tile = pltpu.VMEM((1, blk, blk), BF16)
    row = pltpu.VMEM((1, 1, blk), F32)
    return pl.pallas_call(
        functools.partial(_sb_body, blk=blk),
        grid=(b, ngroup, nq),
        in_specs=[
            pl.BlockSpec((blk, GROUP), lambda bi, g, qi: (bi * nq + qi, g)),
            pl.BlockSpec((s, GROUP), lambda bi, g, qi: (bi, g)),
            pl.BlockSpec((1, GROUP, s), lambda bi, g, qi: (bi, g, 0)),
            pl.BlockSpec((1, GROUP), lambda bi, g, qi: (0, g)),
        ],
        out_specs=pl.BlockSpec((blk, GROUP), lambda bi, g, qi: (bi * nq + qi, g)),
        out_shape=jax.ShapeDtypeStruct((t, SB_WIDTH), BF16),
        scratch_shapes=[
            pltpu.VMEM((nh, HEAD_DIM, blk), F32),
            pltpu.VMEM((nh, blk, GROUP), BF16),
        ] + [tile, tile, tile, tile, row] * nh,
        compiler_params=pltpu.CompilerParams(
            dimension_semantics=("arbitrary", "arbitrary", "arbitrary"),
            vmem_limit_bytes=VMEM_LIMIT),
        name="sb_attention",
    )(q, k, vt, sb_g)


def _rwkv_body(rt_ref, at_ref, bt_ref, kt_ref, v_ref, bh_ref, kh_ref, wc_ref, y_ref,
               state_ref, *, lb):
    j = pl.program_id(1)
    w = RWKV_WIDTH
    npair = w // PAIR

    @pl.when(j == 0)
    def _():
        state_ref[...] = jnp.zeros_like(state_ref)

    n2 = 2 * CHUNK
    lane = lax.broadcasted_iota(jnp.int32, (CHUNK, PAIR), 1)
    first = lane < HEAD_DIM
    br = lax.broadcasted_iota(jnp.int32, (n2, n2), 0)
    bc = lax.broadcasted_iota(jnp.int32, (n2, n2), 1)
    same_head = (br // CHUNK) == (bc // CHUNK)
    strict = same_head & (bc < br)
    incl = same_head & (bc <= br)
    eye = br == bc

    def stack_masked(x):
        return jnp.concatenate([jnp.where(first, x, 0.0), jnp.where(first, 0.0, x)], axis=0)

    def stack(x):
        return jnp.concatenate([x, x], axis=0)

    per_step = 2 if (lb // CHUNK) % 2 == 0 else 1

    def chunk_step(c, carry):
        t0 = pl.multiple_of(c * (per_step * CHUNK), per_step * CHUNK)
        njob = per_step * npair
        sls = [(0, pl.ds(t0 + (jb // npair) * CHUNK, CHUNK),
                slice((jb % npair) * PAIR, (jb % npair + 1) * PAIR)) for jb in range(njob)]
        each = lambda f: [f(jb) for jb in range(njob)]
        a_sm = each(lambda pi: stack_masked(at_ref[sls[pi]]))
        r_sm = each(lambda pi: stack_masked(rt_ref[sls[pi]]))
        v_sm = each(lambda pi: stack_masked(v_ref[sls[pi]]))
        bh_sm = each(lambda pi: stack_masked(bh_ref[sls[pi]]))
        kh_sm = each(lambda pi: stack_masked(kh_ref[sls[pi]]))
        wc = each(lambda pi: wc_ref[
            0, pl.ds(pl.multiple_of((c * per_step + pi // npair) * 8, 8), 8), sls[pi][2]][0:1, :])
        mm = each(lambda pi: _dot(
            jnp.concatenate([a_sm[pi], r_sm[pi]], axis=0),
            jnp.concatenate([stack(bt_ref[sls[pi]]), stack(kt_ref[sls[pi]])], axis=0),
            _NT))
        m_ab = each(lambda pi: jnp.where(strict, mm[pi][:n2, :n2], 0.0))
        m_ak = each(lambda pi: jnp.where(strict, mm[pi][:n2, n2:], 0.0).astype(BF16))
        m_rbk = each(lambda pi: jnp.concatenate(
            [jnp.where(incl, mm[pi][n2:, :n2], 0.0), jnp.where(incl, mm[pi][n2:, n2:], 0.0)],
            axis=1).astype(BF16))
        x = each(lambda pi: jnp.concatenate(
            [a_sm[pi].astype(F32), _dot(m_ak[pi], v_sm[pi])], axis=1))
        lp = m_ab
        steps = CHUNK.bit_length() - 1
        for i in range(steps):
            lp16 = each(lambda pi: lp[pi].astype(BF16))
            if i < steps - 1:
                both = each(lambda pi: _dot(
                    lp16[pi], jnp.concatenate([lp16[pi], x[pi].astype(BF16)], axis=1)))
                lp = each(lambda pi: both[pi][:, :n2])
                x = each(lambda pi: x[pi] + both[pi][:, n2:])
            else:
                x = each(lambda pi: x[pi] + _dot(lp16[pi], x[pi].astype(BF16)))
        pq = each(lambda pi: x[pi].astype(BF16))
        zeros = jnp.zeros((n2, PAIR), BF16)
        gy = each(lambda pi: _dot(m_rbk[pi], jnp.concatenate(
            [pq[pi], jnp.concatenate([zeros, v_sm[pi]], axis=1)], axis=0)))
        phi = each(lambda pi: _dot(pq[pi][:, :PAIR], bh_sm[pi], _TN) + jnp.where(eye, wc[pi], 0.0))
        psi = each(lambda pi: _dot(jnp.concatenate([pq[pi][:, PAIR:], v_sm[pi]], axis=0),
                                   jnp.concatenate([bh_sm[pi], kh_sm[pi]], axis=0), _TN))
        g_pk = each(lambda pi: (r_sm[pi][:CHUNK] + gy[pi][:CHUNK, :PAIR]
                                + r_sm[pi][CHUNK:] + gy[pi][CHUNK:, :PAIR]).astype(BF16))
        y0 = each(lambda pi: gy[pi][:CHUNK, PAIR:] + gy[pi][CHUNK:, PAIR:])
        phi16 = each(lambda pi: phi[pi].astype(BF16))
        state = [state_ref[pi] for pi in range(npair)]
        ys = [None] * njob
        for ci in range(per_step):
            s16 = [st.astype(BF16) for st in state]
            for pi in range(npair):
                jb = ci * npair + pi
                ys[jb] = _dot(g_pk[jb], s16[pi], _NT) + y0[jb]
            state = [_dot(s16[pi], phi16[ci * npair + pi]) + psi[ci * npair + pi]
                     for pi in range(npair)]
        for jb in range(njob):
            y_ref[sls[jb]] = ys[jb].astype(BF16)
        for pi in range(npair):
            state_ref[pi] = state[pi]
        return carry

    lax.fori_loop(0, lb // (per_step * CHUNK), chunk_step, 0)


def _rwkv(rt, at, bt, kt, v, bh, kh, wc, b, lb):
    t, w = rt.shape
    s = t // b
    seq = lambda x: x.reshape(b, s, w)
    tile = pl.BlockSpec((1, lb, w), lambda bi, j: (bi, j, 0))
    return pl.pallas_call(
        functools.partial(_rwkv_body, lb=lb),
        grid=(b, s // lb),
        in_specs=[tile] * 7 + [pl.BlockSpec((1, lb // 8, w), lambda bi, j: (bi, j, 0))],
        out_specs=tile,
        out_shape=jax.ShapeDtypeStruct((b, s, w), BF16),
        scratch_shapes=[pltpu.VMEM((w // PAIR, PAIR, PAIR), F32)],
        compiler_params=pltpu.CompilerParams(
            dimension_semantics=("arbitrary", "arbitrary"), vmem_limit_bytes=VMEM_LIMIT),
        name="rwkv7",
    )(seq(rt), seq(at), seq(bt), seq(kt), seq(v), seq(bh), seq(kh), wc.reshape(b, s // 8, w))


def _outproj_ffn_body(sb_ref, y_ref, bonus_ref, gate_ref, gng_ref, gnb_ref, x_ref, wa_ref, wb_ref,
                      mg_ref, pre_ref, post_ref, wg_ref, wu_ref, wd_ref, o_ref):
    y = y_ref[...].astype(F32)
    inv = 1.0 / HEAD_DIM
    d = y - _head_sums(y) * inv
    var = _head_sums(d * d) * inv
    yn = d * lax.rsqrt(var + GN_EPS) * gng_ref[...] + gnb_ref[...]
    yg = ((yn + bonus_ref[...].astype(F32)) * gate_ref[...].astype(F32)).astype(BF16)
    m = _dot(sb_ref[...], wa_ref[...]) + _dot(yg, wb_ref[...])
    x = x_ref[...] + _rms(m, mg_ref[...])
    xn = _rms(x, pre_ref[...]).astype(BF16)
    g = _dot(xn, wg_ref[...])
    u = _dot(xn, wu_ref[...])
    h = (g * _sigmoid(g) * u).astype(BF16)
    f = _dot(h, wd_ref[...])
    o_ref[...] = x + 0.5 * _rms(f, post_ref[...])


def _outproj_ffn(sb2d, y2d, bonus, gate, gn_g, gn_b, x2d, w_out, mix_g, pre_g, post_g,
                 wg, wu, wd, tm):
    t, d = x2d.shape
    dff = wg.shape[1]
    const = lambda i: (0, 0)
    row = lambda i: (i, 0)
    once = dict(pipeline_mode=pl.Buffered(1))
    return pl.pallas_call(
        _outproj_ffn_body,
        grid=(t // tm,),
        in_specs=[
            pl.BlockSpec((tm, SB_WIDTH), row),
            pl.BlockSpec((tm, RWKV_WIDTH), row),
            pl.BlockSpec((tm, RWKV_WIDTH), row),
            pl.BlockSpec((tm, RWKV_WIDTH), row),
            pl.BlockSpec((1, RWKV_WIDTH), const),
            pl.BlockSpec((1, RWKV_WIDTH), const),
            pl.BlockSpec((tm, d), row),
            pl.BlockSpec((SB_WIDTH, d), lambda i: (0, 0), **once),
            pl.BlockSpec((RWKV_WIDTH, d), lambda i: (1, 0), **once),
            pl.BlockSpec((1, d), const),
            pl.BlockSpec((1, d), const),
            pl.BlockSpec((1, d), const),
            pl.BlockSpec((d, dff), const, **once),
            pl.BlockSpec((d, dff), const, **once),
            pl.BlockSpec((dff, d), const, **once),
        ],
        out_specs=pl.BlockSpec((tm, d), row),
        out_shape=jax.ShapeDtypeStruct((t, d), F32),
        compiler_params=pltpu.CompilerParams(
            dimension_semantics=("arbitrary",), vmem_limit_bytes=VMEM_LIMIT),
        name="outproj_ffn",
    )(sb2d, y2d, bonus, gate, gn_g, gn_b, x2d, w_out, w_out, mix_g, pre_g, post_g, wg, wu, wd)


def _pick(n, pref):
    return pref if n % pref == 0 else n


def kernel(x, ffn1_pre_g, ffn1_post_g, ffn1_w_gate, ffn1_w_up, ffn1_w_down, mix_pre_g, mix_post_g, w_in, shift_mu, sb_out_g, decay_w0, decay_w2, iclr_a0, iclr_a2, gate_w2, k_k, k_a, r_k, gn_g, gn_b, w_out, ffn2_pre_g, ffn2_post_g, ffn2_w_gate, ffn2_w_up, ffn2_w_down):
    b, s, d = x.shape
    depth = ffn1_pre_g.shape[0]
    t = b * s
    tm = _pick(s, 512)
    blk = _pick(s, 256)
    lb = _pick(s, 512)
    x2d = x.reshape(t, d)
    for l in range(depth):
        x2d = _ffn(x2d, ffn1_pre_g[l][None], ffn1_post_g[l][None], ffn1_w_gate[l].astype(BF16),
                   ffn1_w_up[l].astype(BF16), ffn1_w_down[l].astype(BF16), tm)
        zpad = jnp.zeros((AAA_LORA, RWKV_WIDTH), F32)
        dw2 = jnp.concatenate([decay_w2[l], zpad], axis=0).astype(BF16)
        aw2 = jnp.concatenate([zpad, iclr_a2[l]], axis=0).astype(BF16)
        q, k, vt, rt, at, bt, kt, v, bh, kh, wc, bonus, gate = _inproj(
            x2d, mix_pre_g[l][None], w_in[l].astype(BF16), shift_mu[l][None], decay_w0[l][None],
            dw2, iclr_a0[l][None], aw2, gate_w2[l].astype(BF16), k_k[l][None], k_a[l][None],
            r_k[l].reshape(1, -1), tm, s)
        sb = _sb_attention(q, k, vt, sb_out_g[l][None], blk)
        y = _rwkv(rt, at, bt, kt, v, bh, kh, wc, b, lb)
        x2d = _outproj_ffn(sb, y.reshape(t, -1), bonus, gate, gn_g[l][None], gn_b[l][None], x2d,
                           w_out[l].astype(BF16), mix_post_g[l][None],
                           ffn2_pre_g[l][None], ffn2_post_g[l][None], ffn2_w_gate[l].astype(BF16),
                           ffn2_w_up[l].astype(BF16), ffn2_w_down[l].astype(BF16), tm)
    return x2d.reshape(b, s, d)
```

```python
import functools

import jax
import jax.numpy as jnp
from jax import lax
from jax.experimental import pallas as pl
from jax.experimental.pallas import tpu as pltpu

F32 = jnp.float32
BF16 = jnp.bfloat16

HEAD_DIM = 64
SB_HEADS = 8
RWKV_HEADS = 8
SB_WIDTH = SB_HEADS * HEAD_DIM
RWKV_WIDTH = RWKV_HEADS * HEAD_DIM
DECAY_LORA = 64
AAA_LORA = 64
GATE_LORA = 128
NORM_EPS = 1e-6
GN_EPS = HEAD_DIM * 1e-5

LANES = 128
PAIR = LANES
GROUP = 2 * LANES
LOG2E = 1.4426950408889634
LOG2E_BF16 = (1.4453125, -0.00262451171875, 7.033348083496094e-06)
CHUNK = 64
VMEM_LIMIT = 56 * 1024 * 1024

_NT = (((1,), (1,)), ((), ()))
_TN = (((0,), (0,)), ((), ()))


def _dot(a, b, dims=None):
    if dims is None:
        dims = (((a.ndim - 1,), (0,)), ((), ()))
    return lax.dot_general(a, b, dims, preferred_element_type=F32)


def _rms(x, g):
    return x * lax.rsqrt(jnp.mean(x * x, axis=-1, keepdims=True) + NORM_EPS) * g


def _softplus(x):
    return jnp.maximum(x, 0.0) + jnp.log(1.0 + jnp.exp(-jnp.abs(x)))


def _sigmoid(x):
    return 1.0 / (1.0 + jnp.exp(-x))


def _split(x):
    hi = x.astype(BF16)
    return hi, (x - hi.astype(F32)).astype(BF16)


def _head_sums(x):
    r = lax.broadcasted_iota(jnp.int32, (GROUP, GROUP), 0) // HEAD_DIM
    c = lax.broadcasted_iota(jnp.int32, (GROUP, GROUP), 1) // HEAD_DIM
    ones = (r == c).astype(BF16)
    parts = [_dot(x[:, gi * GROUP:(gi + 1) * GROUP].astype(BF16), ones)
             for gi in range(x.shape[1] // GROUP)]
    return jnp.concatenate(parts, axis=1)


def _ffn_body(x_ref, pre_ref, post_ref, wg_ref, wu_ref, wd_ref, o_ref):
    x = x_ref[...]
    xn = _rms(x, pre_ref[...]).astype(BF16)
    g = _dot(xn, wg_ref[...])
    u = _dot(xn, wu_ref[...])
    h = (g * _sigmoid(g) * u).astype(BF16)
    f = _dot(h, wd_ref[...])
    o_ref[...] = x + 0.5 * _rms(f, post_ref[...])


def _ffn(x2d, pre_g, post_g, wg, wu, wd, tm):
    t, d = x2d.shape
    dff = wg.shape[1]
    const = lambda i: (0, 0)
    return pl.pallas_call(
        _ffn_body,
        grid=(t // tm,),
        in_specs=[
            pl.BlockSpec((tm, d), lambda i: (i, 0)),
            pl.BlockSpec((1, d), const),
            pl.BlockSpec((1, d), const),
            pl.BlockSpec((d, dff), const, pipeline_mode=pl.Buffered(1)),
            pl.BlockSpec((d, dff), const, pipeline_mode=pl.Buffered(1)),
            pl.BlockSpec((dff, d), const, pipeline_mode=pl.Buffered(1)),
        ],
        out_specs=pl.BlockSpec((tm, d), lambda i: (i, 0)),
        out_shape=jax.ShapeDtypeStruct((t, d), F32),
        compiler_params=pltpu.CompilerParams(
            dimension_semantics=("arbitrary",), vmem_limit_bytes=VMEM_LIMIT),
        name="ffn",
    )(x2d, pre_g, post_g, wg, wu, wd)


def _inproj_body(x_ref, g_ref, w_ref, mu_ref, w0_ref, dw2_ref, a0_ref, aw2_ref, gw2_ref,
                 kk_ref, ka_ref, rk_ref,
                 q_ref, k_ref, vt_ref, rt_ref, at_ref, bt_ref, kt_ref, v_ref, bh_ref, kh_ref,
                 wc_ref, bonus_ref, gate_ref, last_ref, *, per_seq):
    i = pl.program_id(0)
    w = RWKV_WIDTH
    nsb = 3 * SB_WIDTH
    tm = x_ref.shape[0]

    @pl.when(i % per_seq == 0)
    def _():
        last_ref[...] = jnp.zeros_like(last_ref)

    xn = _rms(x_ref[...], g_ref[...]).astype(BF16)
    p = _dot(xn, w_ref[:, nsb:])
    psb = _dot(xn, w_ref[:, :nsb])
    q_ref[...] = (psb[:, :SB_WIDTH] * (HEAD_DIM ** -0.5 * LOG2E)).astype(BF16)
    k_ref[...] = psb[:, SB_WIDTH:2 * SB_WIDTH].astype(BF16)
    vt_ref[0] = psb[:, 2 * SB_WIDTH:].T.astype(BF16)

    rows = lax.broadcasted_iota(jnp.int32, p.shape, 0)
    prev = jnp.where(rows == 0, last_ref[...], pltpu.roll(p, 1, 0))
    last_ref[...] = p[tm - 1:tm, :]
    xs = p + (prev - p) * mu_ref[...]
    r = xs[:, 0:w]
    kr = xs[:, w:2 * w]
    vr = xs[:, 2 * w:3 * w]
    lora_in = xs[:, 3 * w:3 * w + DECAY_LORA + AAA_LORA]
    gd = xs[:, 3 * w + DECAY_LORA + AAA_LORA:]
    lora_w = _dot(jnp.tanh(lora_in).astype(BF16), dw2_ref[...])
    lora_a = _dot(lora_in.astype(BF16), aw2_ref[...])
    gate_ref[...] = _dot(_sigmoid(gd).astype(BF16), gw2_ref[...]).astype(BF16)
    w_log = -_softplus(-(w0_ref[...] + lora_w)) - 0.5
    lw2 = jnp.exp(w_log) * (-LOG2E)
    a = _sigmoid(a0_ref[...] + lora_a)
    kk = kr * kk_ref[...]
    k2 = kr * (1.0 + (a - 1.0) * ka_ref[...])
    bonus_ref[...] = (_head_sums(r * k2 * rk_ref[...]) * vr).astype(BF16)
    kkn = kk * lax.rsqrt(jnp.maximum(_head_sums(kk * kk), 1e-24))
    a_vec = -kkn
    b_vec = kkn * a
    v_ref[...] = vr.astype(BF16)

    sub = 4 * CHUNK
    tr = lax.broadcasted_iota(jnp.int32, (sub, sub), 0)
    tc = lax.broadcasted_iota(jnp.int32, (sub, sub), 1)
    prefix = (((tr // CHUNK) == (tc // CHUNK)) & (tc <= tr)).astype(BF16)
    hi, lo = _split(lw2)
    cums, tots, ends = [], [], []
    for s0 in range(0, tm, sub):
        cum = _dot(prefix, hi[s0:s0 + sub]) + _dot(prefix, lo[s0:s0 + sub])
        cums.append(cum)
        for c in range(sub // CHUNK):
            end = cum[c * CHUNK + CHUNK - 1:(c + 1) * CHUNK, :]
            ends.append(jnp.broadcast_to(end, (8, w)))
            tots.append(jnp.broadcast_to(end, (CHUNK, w)))
    cum = jnp.concatenate(cums, axis=0)
    tot = jnp.concatenate(tots, axis=0)
    dec_out = jnp.exp2(-cum)
    dec_rest = jnp.exp2(tot - cum)
    rt_ref[...] = (r * jnp.exp2(cum)).astype(BF16)
    at_ref[...] = (a_vec * jnp.exp2(cum - lw2)).astype(BF16)
    bt_ref[...] = (b_vec * dec_out).astype(BF16)
    kt_ref[...] = (k2 * dec_out).astype(BF16)
    bh_ref[...] = (b_vec * dec_rest).astype(BF16)
    kh_ref[...] = (k2 * dec_rest).astype(BF16)
    wc_ref[...] = jnp.exp2(jnp.concatenate(ends, axis=0))


def _inproj(x2d, g, w_in, mu, w0, dw2, a0, aw2, gw2, k_k, k_a, r_k, tm, seq):
    t, d = x2d.shape
    n = w_in.shape[1]
    w = RWKV_WIDTH
    per_seq = seq // tm
    const = lambda i: (0, 0)
    row = lambda i: (i, 0)
    vec = lambda m: pl.BlockSpec((1, m), const)
    tile = pl.BlockSpec((tm, w), row)
    wide = jax.ShapeDtypeStruct((t, w), BF16)
    return pl.pallas_call(
        functools.partial(_inproj_body, per_seq=per_seq),
        grid=(t // tm,),
        in_specs=[
            pl.BlockSpec((tm, d), row),
            vec(d),
            pl.BlockSpec((d, n), const, pipeline_mode=pl.Buffered(1)),
            vec(n - 3 * SB_WIDTH), vec(w),
            pl.BlockSpec((DECAY_LORA + AAA_LORA, w), const),
            vec(w),
            pl.BlockSpec((DECAY_LORA + AAA_LORA, w), const),
            pl.BlockSpec((GATE_LORA, w), const),
            vec(w), vec(w), vec(w),
        ],
        out_specs=[
            pl.BlockSpec((tm, SB_WIDTH), row),
            pl.BlockSpec((tm, SB_WIDTH), row),
            pl.BlockSpec((1, SB_WIDTH, tm), lambda i: (i // per_seq, 0, i % per_seq)),
            tile, tile, tile, tile, tile, tile, tile,
            pl.BlockSpec((tm // 8, w), row),
            tile, tile,
        ],
        out_shape=[
            jax.ShapeDtypeStruct((t, SB_WIDTH), BF16),
            jax.ShapeDtypeStruct((t, SB_WIDTH), BF16),
            jax.ShapeDtypeStruct((t // seq, SB_WIDTH, seq), BF16),
            wide, wide, wide, wide, wide, wide, wide,
            jax.ShapeDtypeStruct((t // 8, w), F32),
            wide, wide,
        ],
        scratch_shapes=[pltpu.VMEM((1, n - 3 * SB_WIDTH), F32)],
        compiler_params=pltpu.CompilerParams(
            dimension_semantics=("arbitrary",), vmem_limit_bytes=VMEM_LIMIT),
        name="inproj",
    )(x2d, g, w_in, mu, w0, dw2, a0, aw2, gw2, k_k, k_a, r_k)


def _sb_body(q_ref, k_ref, vt_ref, g_ref, o_ref, acc_ref, qh_ref, *stage_refs, blk):
    qi = pl.program_id(2)
    nh = GROUP // HEAD_DIM
    halves = tuple((h,) for h in range(nh))
    names = ("zraw", "craw", "z", "sp", "tot")
    bufs = tuple(dict(zip(names, stage_refs[len(names) * h:len(names) * (h + 1)]))
                 for h in range(nh))

    q = q_ref[...]
    lane = lax.broadcasted_iota(jnp.int32, (blk, GROUP), 1)
    zero = jnp.zeros_like(q)
    for h in range(nh):
        qh_ref[h] = jnp.where(lane // HEAD_DIM == h, q, zero)
    key = lax.broadcasted_iota(jnp.int32, (blk, blk), 0)
    qry = lax.broadcasted_iota(jnp.int32, (blk, blk), 1)
    neg_after = jnp.where(qry > key, -1.0, 0.0).astype(BF16)
    causal = key < qry

    def rows_of(kb):
        return pl.ds(pl.multiple_of(kb * blk, blk), blk)

    def part(ew=None, mx=None):
        if mx is not None:
            mhalf, kb_s, do_sums = mx
            mb = bufs[mhalf]
            if do_sums:
                sums = [_dot(neg_after, mb["sp"][i]) for i in range(len(halves[mhalf]))]
            if kb_s is not None:
                kblk = k_ref[rows_of(kb_s), :]
                raw = [_dot(kblk, qh_ref[h], _NT) for h in halves[mhalf]]
        new = None
        if ew is not None:
            ehalf, kb_w, do_score, diagonal, carries = ew
            eb = bufs[ehalf]
            new = list(carries)
            outs = []
            if kb_w is not None:
                for i, h in enumerate(halves[ehalf]):
                    new[h] = eb["tot"][i] + carries[h]
                    w = jnp.exp2(eb["z"][i] + eb["craw"][i])
                    vt = vt_ref[0, h * HEAD_DIM:(h + 1) * HEAD_DIM, rows_of(kb_w)]
                    outs.append(_dot(vt, w) * jnp.exp2(carries[h]))
            if do_score:
                for i in range(len(halves[ehalf])):
                    z = eb["zraw"][i]
                    if diagonal:
                        z = jnp.where(causal, z, -1e30)
                    m = jnp.maximum(z, 0.0)
                    ln = jnp.log(1.0 + jnp.exp2(-jnp.abs(z)))
                    soft = ln * LOG2E_BF16[0] + (ln * LOG2E_BF16[1] + ln * LOG2E_BF16[2])
                    eb["z"][i] = (z - m) - soft
                    eb["sp"][i] = m + soft
        if mx is not None:
            for i in range(len(halves[mhalf])):
                if do_sums:
                    mb["tot"][i] = sums[i][0:1, :] - mb["sp"][i][0:1, :].astype(F32)
                    mb["craw"][i] = sums[i].astype(BF16)
                if kb_s is not None:
                    mb["zraw"][i] = raw[i].astype(BF16)
        if ew is not None and kb_w is not None:
            for i, h in enumerate(halves[ehalf]):
                acc_ref[h] += outs[i]
        return None if new is None else tuple(new)

    acc_ref[...] = jnp.zeros_like(acc_ref)
    carries = (jnp.zeros((1, blk), F32),) * nh
    lead = nh // 2

    def one_step(kb_w, kb_s, diagonal, carries, last=False):
        for h in range(nh):
            ahead = h + lead
            if ahead < nh:
                mx = (ahead, kb_s, kb_w is not None)
            elif last:
                mx = None
            else:
                mx = (ahead - nh, jnp.maximum(kb_s - 1, 0), True)
            carries = part(ew=(h, kb_w, kb_s is not None, diagonal, carries), mx=mx)
        return carries

    for h in range(lead):
        part(mx=(h, qi, False))
    carries = one_step(None, qi, True, carries)
    odd = qi % 2
    carries = lax.cond(odd == 1, lambda cs: one_step(qi, qi - 1, False, cs), lambda cs: cs,
                       carries)

    def two_steps(j, cs):
        kb_w = qi - odd - 2 * j
        cs = one_step(kb_w, kb_w - 1, False, cs)
        return one_step(kb_w - 1, kb_w - 2, False, cs)

    carries = lax.fori_loop(0, qi // 2, two_steps, carries)
    one_step(0, None, False, carries, last=True)

    parts = []
    for h in range(nh):
        a = acc_ref[h]
        ms = jnp.mean(a * a, axis=0, keepdims=True)
        parts.append(a * lax.rsqrt(ms + NORM_EPS))
    out = jnp.concatenate(parts, axis=0).T
    o_ref[...] = (out * g_ref[...]).astype(BF16)


def _sb_attention(q, k, vt, sb_g, blk):
    t = q.shape[0]
    b, _, s = vt.shape
    ngroup = SB_WIDTH // GROUP
    nq = s // blk
    nh = GROUP // HEAD_DIM
    tile = pltpu.VMEM((1, blk, blk), BF16)
    row = pltpu.VMEM((1, 1, blk), F32)
    return pl.pallas_call(
        functools.partial(_sb_body, blk=blk),
        grid=(b, ngroup, nq),
        in_specs=[
            pl.BlockSpec((blk, GROUP), lambda bi, g, qi: (bi * nq + qi, g)),
            pl.BlockSpec((s, GROUP), lambda bi, g, qi: (bi, g)),
            pl.BlockSpec((1, GROUP, s), lambda bi, g, qi: (bi, g, 0)),
            pl.BlockSpec((1, GROUP), lambda bi, g, qi: (0, g)),
        ],
        out_specs=pl.BlockSpec((blk, GROUP), lambda bi, g, qi: (bi * nq + qi, g)),
        out_shape=jax.ShapeDtypeStruct((t, SB_WIDTH), BF16),
        scratch_shapes=[
            pltpu.VMEM((nh, HEAD_DIM, blk), F32),
            pltpu.VMEM((nh, blk, GROUP), BF16),
        ] + [tile, tile, tile, tile, row] * nh,
        compiler_params=pltpu.CompilerParams(
            dimension_semantics=("arbitrary", "arbitrary", "arbitrary"),
            vmem_limit_bytes=VMEM_LIMIT),
        name="sb_attention",
    )(q, k, vt, sb_g)


def _rwkv_body(rt_ref, at_ref, bt_ref, kt_ref, v_ref, bh_ref, kh_ref, wc_ref, y_ref,
               state_ref, *, lb):
    j = pl.program_id(1)
    w = RWKV_WIDTH
    npair = w // PAIR

    @pl.when(j == 0)
    def _():
        state_ref[...] = jnp.zeros_like(state_ref)

    n2 = 2 * CHUNK
    lane = lax.broadcasted_iota(jnp.int32, (CHUNK, PAIR), 1)
    first = lane < HEAD_DIM
    br = lax.broadcasted_iota(jnp.int32, (n2, n2), 0)
    bc = lax.broadcasted_iota(jnp.int32, (n2, n2), 1)
    same_head = (br // CHUNK) == (bc // CHUNK)
    strict = same_head & (bc < br)
    incl = same_head & (bc <= br)
    eye = br == bc

    def stack_masked(x):
        return jnp.concatenate([jnp.where(first, x, 0.0), jnp.where(first, 0.0, x)], axis=0)

    def stack(x):
        return jnp.concatenate([x, x], axis=0)

    per_step = 2 if (lb // CHUNK) % 2 == 0 else 1

    def chunk_step(c, carry):
        t0 = pl.multiple_of(c * (per_step * CHUNK), per_step * CHUNK)
        njob = per_step * npair
        sls = [(0, pl.ds(t0 + (jb // npair) * CHUNK, CHUNK),
                slice((jb % npair) * PAIR, (jb % npair + 1) * PAIR)) for jb in range(njob)]
        each = lambda f: [f(jb) for jb in range(njob)]
        a_sm = each(lambda pi: stack_masked(at_ref[sls[pi]]))
        r_sm = each(lambda pi: stack_masked(rt_ref[sls[pi]]))
        v_sm = each(lambda pi: stack_masked(v_ref[sls[pi]]))
        bh_sm = each(lambda pi: stack_masked(bh_ref[sls[pi]]))
        kh_sm = each(lambda pi: stack_masked(kh_ref[sls[pi]]))
        wc = each(lambda pi: wc_ref[
            0, pl.ds(pl.multiple_of((c * per_step + pi // npair) * 8, 8), 8), sls[pi][2]][0:1, :])
        mm = each(lambda pi: _dot(
            jnp.concatenate([a_sm[pi], r_sm[pi]], axis=0),
            jnp.concatenate([stack(bt_ref[sls[pi]]), stack(kt_ref[sls[pi]])], axis=0),
            _NT))
        m_ab = each(lambda pi: jnp.where(strict, mm[pi][:n2, :n2], 0.0))
        m_ak = each(lambda pi: jnp.where(strict, mm[pi][:n2, n2:], 0.0).astype(BF16))
        m_rbk = each(lambda pi: jnp.concatenate(
            [jnp.where(incl, mm[pi][n2:, :n2], 0.0), jnp.where(incl, mm[pi][n2:, n2:], 0.0)],
            axis=1).astype(BF16))
        rhs = each(lambda pi: jnp.concatenate(
            [a_sm[pi], _dot(m_ak[pi], v_sm[pi]).astype(BF16)], axis=1))
        inv = each(lambda pi: jnp.where(eye, 1.0, m_ab[pi]))
        lp16 = each(lambda pi: m_ab[pi].astype(BF16))
        lp16 = each(lambda pi: _dot(lp16[pi], lp16[pi]).astype(BF16))
        steps = CHUNK.bit_length() - 1
        for i in range(1, steps):
            if i < steps - 1:
                both = each(lambda pi: _dot(
                    lp16[pi], jnp.concatenate([lp16[pi], inv[pi].astype(BF16)], axis=1)))
                lp16 = each(lambda pi: both[pi][:, :n2].astype(BF16))
                inv = each(lambda pi: inv[pi] + both[pi][:, n2:])
            else:
                inv = each(lambda pi: inv[pi] + _dot(lp16[pi], inv[pi].astype(BF16)))
        pq = each(lambda pi: _dot(inv[pi].astype(BF16), rhs[pi]).astype(BF16))
        zeros = jnp.zeros((n2, PAIR), BF16)
        gy = each(lambda pi: _dot(m_rbk[pi], jnp.concatenate(
            [pq[pi], jnp.concatenate([zeros, v_sm[pi]], axis=1)], axis=0)))
        phi = each(lambda pi: _dot(pq[pi][:, :PAIR], bh_sm[pi], _TN) + jnp.where(eye, wc[pi], 0.0))
        psi = each(lambda pi: _dot(jnp.concatenate([pq[pi][:, PAIR:], v_sm[pi]], axis=0),
                                   jnp.concatenate([bh_sm[pi], kh_sm[pi]], axis=0), _TN))
        g_pk = each(lambda pi: (r_sm[pi][:CHUNK] + gy[pi][:CHUNK, :PAIR]
                                + r_sm[pi][CHUNK:] + gy[pi][CHUNK:, :PAIR]).astype(BF16))
        y0 = each(lambda pi: gy[pi][:CHUNK, PAIR:] + gy[pi][CHUNK:, PAIR:])
        phi16 = each(lambda pi: phi[pi].astype(BF16))
        state = [state_ref[pi] for pi in range(npair)]
        ys = [None] * njob
        for ci in range(per_step):
            s16 = [st.astype(BF16) for st in state]
            for pi in range(npair):
                jb = ci * npair + pi
                ys[jb] = _dot(g_pk[jb], s16[pi], _NT) + y0[jb]
            state = [_dot(s16[pi], phi16[ci * npair + pi]) + psi[ci * npair + pi]
                     for pi in range(npair)]
        for jb in range(njob):
            y_ref[sls[jb]] = ys[jb].astype(BF16)
        for pi in range(npair):
            state_ref[pi] = state[pi]
        return carry

    lax.fori_loop(0, lb // (per_step * CHUNK), chunk_step, 0)


def _rwkv(rt, at, bt, kt, v, bh, kh, wc, b, lb):
    t, w = rt.shape
    s = t // b
    seq = lambda x: x.reshape(b, s, w)
    tile = pl.BlockSpec((1, lb, w), lambda bi, j: (bi, j, 0))
    return pl.pallas_call(
        functools.partial(_rwkv_body, lb=lb),
        grid=(b, s // lb),
        in_specs=[tile] * 7 + [pl.BlockSpec((1, lb // 8, w), lambda bi, j: (bi, j, 0))],
        out_specs=tile,
        out_shape=jax.ShapeDtypeStruct((b, s, w), BF16),
        scratch_shapes=[pltpu.VMEM((w // PAIR, PAIR, PAIR), F32)],
        compiler_params=pltpu.CompilerParams(
            dimension_semantics=("arbitrary", "arbitrary"), vmem_limit_bytes=VMEM_LIMIT),
        name="rwkv7",
    )(seq(rt), seq(at), seq(bt), seq(kt), seq(v), seq(bh), seq(kh), wc.reshape(b, s // 8, w))


def _outproj_ffn_body(sb_ref, y_ref, bonus_ref, gate_ref, gng_ref, gnb_ref, x_ref, wa_ref, wb_ref,
                      mg_ref, pre_ref, post_ref, wg_ref, wu_ref, wd_ref, o_ref):
    y = y_ref[...].astype(F32)
    inv = 1.0 / HEAD_DIM
    d = y - _head_sums(y) * inv
    var = _head_sums(d * d) * inv
    yn = d * lax.rsqrt(var + GN_EPS) * gng_ref[...] + gnb_ref[...]
    yg = ((yn + bonus_ref[...].astype(F32)) * gate_ref[...].astype(F32)).astype(BF16)
    m = _dot(sb_ref[...], wa_ref[...]) + _dot(yg, wb_ref[...])
    x = x_ref[...] + _rms(m, mg_ref[...])
    xn = _rms(x, pre_ref[...]).astype(BF16)
    g = _dot(xn, wg_ref[...])
    u = _dot(xn, wu_ref[...])
    h = (g * _sigmoid(g) * u).astype(BF16)
    f = _dot(h, wd_ref[...])
    o_ref[...] = x + 0.5 * _rms(f, post_ref[...])


def _outproj_ffn(sb2d, y2d, bonus, gate, gn_g, gn_b, x2d, w_out, mix_g, pre_g, post_g,
                 wg, wu, wd, tm):
    t, d = x2d.shape
    dff = wg.shape[1]
    const = lambda i: (0, 0)
    row = lambda i: (i, 0)
    once = dict(pipeline_mode=pl.Buffered(1))
    return pl.pallas_call(
        _outproj_ffn_body,
        grid=(t // tm,),
        in_specs=[
            pl.BlockSpec((tm, SB_WIDTH), row),
            pl.BlockSpec((tm, RWKV_WIDTH), row),
            pl.BlockSpec((tm, RWKV_WIDTH), row),
            pl.BlockSpec((tm, RWKV_WIDTH), row),
            pl.BlockSpec((1, RWKV_WIDTH), const),
            pl.BlockSpec((1, RWKV_WIDTH), const),
            pl.BlockSpec((tm, d), row),
            pl.BlockSpec((SB_WIDTH, d), lambda i: (0, 0), **once),
            pl.BlockSpec((RWKV_WIDTH, d), lambda i: (1, 0), **once),
            pl.BlockSpec((1, d), const),
            pl.BlockSpec((1, d), const),
            pl.BlockSpec((1, d), const),
            pl.BlockSpec((d, dff), const, **once),
            pl.BlockSpec((d, dff), const, **once),
            pl.BlockSpec((dff, d), const, **once),
        ],
        out_specs=pl.BlockSpec((tm, d), row),
        out_shape=jax.ShapeDtypeStruct((t, d), F32),
        compiler_params=pltpu.CompilerParams(
            dimension_semantics=("arbitrary",), vmem_limit_bytes=VMEM_LIMIT),
        name="outproj_ffn",
    )(sb2d, y2d, bonus, gate, gn_g, gn_b, x2d, w_out, w_out, mix_g, pre_g, post_g, wg, wu, wd)


def _pick(n, pref):
    return pref if n % pref == 0 else n


def kernel(x, ffn1_pre_g, ffn1_post_g, ffn1_w_gate, ffn1_w_up, ffn1_w_down, mix_pre_g, mix_post_g, w_in, shift_mu, sb_out_g, decay_w0, decay_w2, iclr_a0, iclr_a2, gate_w2, k_k, k_a, r_k, gn_g, gn_b, w_out, ffn2_pre_g, ffn2_post_g, ffn2_w_gate, ffn2_w_up, ffn2_w_down):
    b, s, d = x.shape
    depth = ffn1_pre_g.shape[0]
    t = b * s
    tm = _pick(s, 512)
    blk = _pick(s, 256)
    lb = _pick(s, 512)
    x2d = x.reshape(t, d)
    for l in range(depth):
        x2d = _ffn(x2d, ffn1_pre_g[l][None], ffn1_post_g[l][None], ffn1_w_gate[l].astype(BF16),
                   ffn1_w_up[l].astype(BF16), ffn1_w_down[l].astype(BF16), tm)
        zpad = jnp.zeros((AAA_LORA, RWKV_WIDTH), F32)
        dw2 = jnp.concatenate([decay_w2[l], zpad], axis=0).astype(BF16)
        aw2 = jnp.concatenate([zpad, iclr_a2[l]], axis=0).astype(BF16)
        q, k, vt, rt, at, bt, kt, v, bh, kh, wc, bonus, gate = _inproj(
            x2d, mix_pre_g[l][None], w_in[l].astype(BF16), shift_mu[l][None], decay_w0[l][None],
            dw2, iclr_a0[l][None], aw2, gate_w2[l].astype(BF16), k_k[l][None], k_a[l][None],
            r_k[l].reshape(1, -1), tm, s)
        sb = _sb_attention(q, k, vt, sb_out_g[l][None], blk)
        y = _rwkv(rt, at, bt, kt, v, bh, kh, wc, b, lb)
        x2d = _outproj_ffn(sb, y.reshape(t, -1), bonus, gate, gn_g[l][None], gn_b[l][None], x2d,
                           w_out[l].astype(BF16), mix_post_g[l][None],
                           ffn2_pre_g[l][None], ffn2_post_g[l][None], ffn2_w_gate[l].astype(BF16),
                           ffn2_w_up[l].astype(BF16), ffn2_w_down[l].astype(BF16), tm)
    return x2d.reshape(b, s, d)
```

```python
import functools

import jax
import jax.numpy as jnp
from jax import lax
from jax.experimental import pallas as pl
from jax.experimental.pallas import tpu as pltpu

F32 = jnp.float32
BF16 = jnp.bfloat16

HEAD_DIM = 64
SB_HEADS = 8
RWKV_HEADS = 8
SB_WIDTH = SB_HEADS * HEAD_DIM
RWKV_WIDTH = RWKV_HEADS * HEAD_DIM
DECAY_LORA = 64
AAA_LORA = 64
GATE_LORA = 128
NORM_EPS = 1e-6
GN_EPS = HEAD_DIM * 1e-5

LANES = 128
PAIR = LANES
GROUP = 2 * LANES
LOG2E = 1.4426950408889634
LOG2E_BF16 = (1.4453125, -0.00262451171875, 7.033348083496094e-06)
CHUNK = 64
VMEM_LIMIT = 56 * 1024 * 1024

_NT = (((1,), (1,)), ((), ()))
_TN = (((0,), (0,)), ((), ()))


def _dot(a, b, dims=None):
    if dims is None:
        dims = (((a.ndim - 1,), (0,)), ((), ()))
    return lax.dot_general(a, b, dims, preferred_element_type=F32)


def _rms(x, g):
    return x * lax.rsqrt(jnp.mean(x * x, axis=-1, keepdims=True) + NORM_EPS) * g


def _softplus(x):
    return jnp.maximum(x, 0.0) + jnp.log(1.0 + jnp.exp(-jnp.abs(x)))


def _sigmoid(x):
    return 1.0 / (1.0 + jnp.exp(-x))


def _split(x):
    hi = x.astype(BF16)
    return hi, (x - hi.astype(F32)).astype(BF16)


def _head_sums(x):
    r = lax.broadcasted_iota(jnp.int32, (GROUP, GROUP), 0) // HEAD_DIM
    c = lax.broadcasted_iota(jnp.int32, (GROUP, GROUP), 1) // HEAD_DIM
    ones = (r == c).astype(BF16)
    parts = [_dot(x[:, gi * GROUP:(gi + 1) * GROUP].astype(BF16), ones)
             for gi in range(x.shape[1] // GROUP)]
    return jnp.concatenate(parts, axis=1)


def _ffn_body(x_ref, pre_ref, post_ref, wg_ref, wu_ref, wd_ref, o_ref):
    x = x_ref[...]
    xn = _rms(x, pre_ref[...]).astype(BF16)
    g = _dot(xn, wg_ref[...])
    u = _dot(xn, wu_ref[...])
    h = (g * _sigmoid(g) * u).astype(BF16)
    f = _dot(h, wd_ref[...])
    o_ref[...] = x + 0.5 * _rms(f, post_ref[...])


def _ffn(x2d, pre_g, post_g, wg, wu, wd, tm):
    t, d = x2d.shape
    dff = wg.shape[1]
    const = lambda i: (0, 0)
    return pl.pallas_call(
        _ffn_body,
        grid=(t // tm,),
        in_specs=[
            pl.BlockSpec((tm, d), lambda i: (i, 0)),
            pl.BlockSpec((1, d), const),
            pl.BlockSpec((1, d), const),
            pl.BlockSpec((d, dff), const, pipeline_mode=pl.Buffered(1)),
            pl.BlockSpec((d, dff), const, pipeline_mode=pl.Buffered(1)),
            pl.BlockSpec((dff, d), const, pipeline_mode=pl.Buffered(1)),
        ],
        out_specs=pl.BlockSpec((tm, d), lambda i: (i, 0)),
        out_shape=jax.ShapeDtypeStruct((t, d), F32),
        compiler_params=pltpu.CompilerParams(
            dimension_semantics=("arbitrary",), vmem_limit_bytes=VMEM_LIMIT),
        name="ffn",
    )(x2d, pre_g, post_g, wg, wu, wd)


def _inproj_body(x_ref, g_ref, w_ref, mu_ref, w0_ref, dw2_ref, a0_ref, aw2_ref, gw2_ref,
                 kk_ref, ka_ref, rk_ref,
                 q_ref, k_ref, vt_ref, rt_ref, at_ref, bt_ref, kt_ref, v_ref, bh_ref, kh_ref,
                 wc_ref, bonus_ref, gate_ref, last_ref, *, per_seq):
    i = pl.program_id(0)
    w = RWKV_WIDTH
    nsb = 3 * SB_WIDTH
    tm = x_ref.shape[0]

    @pl.when(i % per_seq == 0)
    def _():
        last_ref[...] = jnp.zeros_like(last_ref)

    xn = _rms(x_ref[...], g_ref[...]).astype(BF16)
    p = _dot(xn, w_ref[:, nsb:])
    psb = _dot(xn, w_ref[:, :nsb])
    q_ref[...] = (psb[:, :SB_WIDTH] * (HEAD_DIM ** -0.5 * LOG2E)).astype(BF16)
    k_ref[...] = psb[:, SB_WIDTH:2 * SB_WIDTH].astype(BF16)
    vt_ref[0] = psb[:, 2 * SB_WIDTH:].T.astype(BF16)

    rows = lax.broadcasted_iota(jnp.int32, p.shape, 0)
    prev = jnp.where(rows == 0, last_ref[...], pltpu.roll(p, 1, 0))
    last_ref[...] = p[tm - 1:tm, :]
    xs = p + (prev - p) * mu_ref[...]
    r = xs[:, 0:w]
    kr = xs[:, w:2 * w]
    vr = xs[:, 2 * w:3 * w]
    lora_in = xs[:, 3 * w:3 * w + DECAY_LORA + AAA_LORA]
    gd = xs[:, 3 * w + DECAY_LORA + AAA_LORA:]
    lora_w = _dot(jnp.tanh(lora_in).astype(BF16), dw2_ref[...])
    lora_a = _dot(lora_in.astype(BF16), aw2_ref[...])
    gate_ref[...] = _dot(_sigmoid(gd).astype(BF16), gw2_ref[...]).astype(BF16)
    w_log = -_softplus(-(w0_ref[...] + lora_w)) - 0.5
    lw2 = jnp.exp(w_log) * (-LOG2E)
    a = _sigmoid(a0_ref[...] + lora_a)
    kk = kr * kk_ref[...]
    k2 = kr * (1.0 + (a - 1.0) * ka_ref[...])
    bonus_ref[...] = (_head_sums(r * k2 * rk_ref[...]) * vr).astype(BF16)
    kkn = kk * lax.rsqrt(jnp.maximum(_head_sums(kk * kk), 1e-24))
    a_vec = -kkn
    b_vec = kkn * a
    v_ref[...] = vr.astype(BF16)

    sub = 4 * CHUNK
    tr = lax.broadcasted_iota(jnp.int32, (sub, sub), 0)
    tc = lax.broadcasted_iota(jnp.int32, (sub, sub), 1)
    prefix = (((tr // CHUNK) == (tc // CHUNK)) & (tc <= tr)).astype(BF16)
    hi, lo = _split(lw2)
    cums, tots, ends = [], [], []
    for s0 in range(0, tm, sub):
        cum = _dot(prefix, hi[s0:s0 + sub]) + _dot(prefix, lo[s0:s0 + sub])
        cums.append(cum)
        for c in range(sub // CHUNK):
            end = cum[c * CHUNK + CHUNK - 1:(c + 1) * CHUNK, :]
            ends.append(jnp.broadcast_to(end, (8, w)))
            tots.append(jnp.broadcast_to(end, (CHUNK, w)))
    cum = jnp.concatenate(cums, axis=0)
    tot = jnp.concatenate(tots, axis=0)
    dec_out = jnp.exp2(-cum)
    dec_rest = jnp.exp2(tot - cum)
    rt_ref[...] = (r * jnp.exp2(cum)).astype(BF16)
    at_ref[...] = (a_vec * jnp.exp2(cum - lw2)).astype(BF16)
    bt_ref[...] = (b_vec * dec_out).astype(BF16)
    kt_ref[...] = (k2 * dec_out).astype(BF16)
    bh_ref[...] = (b_vec * dec_rest).astype(BF16)
    kh_ref[...] = (k2 * dec_rest).astype(BF16)
    wc_ref[...] = jnp.exp2(jnp.concatenate(ends, axis=0))


def _inproj(x2d, g, w_in, mu, w0, dw2, a0, aw2, gw2, k_k, k_a, r_k, tm, seq):
    t, d = x2d.shape
    n = w_in.shape[1]
    w = RWKV_WIDTH
    per_seq = seq // tm
    const = lambda i: (0, 0)
    row = lambda i: (i, 0)
    vec = lambda m: pl.BlockSpec((1, m), const)
    tile = pl.BlockSpec((tm, w), row)
    wide = jax.ShapeDtypeStruct((t, w), BF16)
    return pl.pallas_call(
        functools.partial(_inproj_body, per_seq=per_seq),
        grid=(t // tm,),
        in_specs=[
            pl.BlockSpec((tm, d), row),
            vec(d),
            pl.BlockSpec((d, n), const, pipeline_mode=pl.Buffered(1)),
            vec(n - 3 * SB_WIDTH), vec(w),
            pl.BlockSpec((DECAY_LORA + AAA_LORA, w), const),
            vec(w),
            pl.BlockSpec((DECAY_LORA + AAA_LORA, w), const),
            pl.BlockSpec((GATE_LORA, w), const),
            vec(w), vec(w), vec(w),
        ],
        out_specs=[
            pl.BlockSpec((tm, SB_WIDTH), row),
            pl.BlockSpec((tm, SB_WIDTH), row),
            pl.BlockSpec((1, SB_WIDTH, tm), lambda i: (i // per_seq, 0, i % per_seq)),
            tile, tile, tile, tile, tile, tile, tile,
            pl.BlockSpec((tm // 8, w), row),
            tile, tile,
        ],
        out_shape=[
            jax.ShapeDtypeStruct((t, SB_WIDTH), BF16),
            jax.ShapeDtypeStruct((t, SB_WIDTH), BF16),
            jax.ShapeDtypeStruct((t // seq, SB_WIDTH, seq), BF16),
            wide, wide, wide, wide, wide, wide, wide,
            jax.ShapeDtypeStruct((t // 8, w), F32),
            wide, wide,
        ],
        scratch_shapes=[pltpu.VMEM((1, n - 3 * SB_WIDTH), F32)],
        compiler_params=pltpu.CompilerParams(
            dimension_semantics=("arbitrary",), vmem_limit_bytes=VMEM_LIMIT),
        name="inproj",
    )(x2d, g, w_in, mu, w0, dw2, a0, aw2, gw2, k_k, k_a, r_k)


def _mixer_body(q_ref, k_ref, vt_ref, g_ref, rt_ref, at_ref, bt_ref, kt_ref, v_ref, bh_ref,
                kh_ref, wc_ref, o_ref, y_ref, acc_ref, qh_ref, state_ref, *stage_refs, blk):
    qi = pl.program_id(2)
    nh = GROUP // HEAD_DIM
    halves = tuple((h,) for h in range(nh))
    names = ("zraw", "craw", "z", "sp", "tot")
    bufs = tuple(dict(zip(names, stage_refs[len(names) * h:len(names) * (h + 1)]))
                 for h in range(nh))

    q = q_ref[...]
    lane = lax.broadcasted_iota(jnp.int32, (blk, GROUP), 1)
    zero = jnp.zeros_like(q)
    for h in range(nh):
        qh_ref[h] = jnp.where(lane // HEAD_DIM == h, q, zero)
    key = lax.broadcasted_iota(jnp.int32, (blk, blk), 0)
    qry = lax.broadcasted_iota(jnp.int32, (blk, blk), 1)
    neg_after = jnp.where(qry > key, -1.0, 0.0).astype(BF16)
    causal = key < qry

    def rows_of(kb):
        return pl.ds(pl.multiple_of(kb * blk, blk), blk)

    def part(ew=None, mx=None):
        if mx is not None:
            mhalf, kb_s, do_sums = mx
            mb = bufs[mhalf]
            if do_sums:
                sums = [_dot(neg_after, mb["sp"][i]) for i in range(len(halves[mhalf]))]
            if kb_s is not None:
                kblk = k_ref[rows_of(kb_s), :]
                raw = [_dot(kblk, qh_ref[h], _NT) for h in halves[mhalf]]
        new = None
        if ew is not None:
            ehalf, kb_w, do_score, diagonal, carries = ew
            eb = bufs[ehalf]
            new = list(carries)
            outs = []
            if kb_w is not None:
                for i, h in enumerate(halves[ehalf]):
                    new[h] = eb["tot"][i] + carries[h]
                    w = jnp.exp2(eb["z"][i] + eb["craw"][i])
                    vt = vt_ref[0, h * HEAD_DIM:(h + 1) * HEAD_DIM, rows_of(kb_w)]
                    outs.append(_dot(vt, w) * jnp.exp2(carries[h]))
            if do_score:
                for i in range(len(halves[ehalf])):
                    z = eb["zraw"][i]
                    if diagonal:
                        z = jnp.where(causal, z, -1e30)
                    m = jnp.maximum(z, 0.0)
                    ln = jnp.log(1.0 + jnp.exp2(-jnp.abs(z)))
                    soft = ln * LOG2E_BF16[0] + (ln * LOG2E_BF16[1] + ln * LOG2E_BF16[2])
                    eb["z"][i] = (z - m) - soft
                    eb["sp"][i] = m + soft
        if mx is not None:
            for i in range(len(halves[mhalf])):
                if do_sums:
                    mb["tot"][i] = sums[i][0:1, :] - mb["sp"][i][0:1, :].astype(F32)
                    mb["craw"][i] = sums[i].astype(BF16)
                if kb_s is not None:
                    mb["zraw"][i] = raw[i].astype(BF16)
        if ew is not None and kb_w is not None:
            for i, h in enumerate(halves[ehalf]):
                acc_ref[h] += outs[i]
        return None if new is None else tuple(new)

    acc_ref[...] = jnp.zeros_like(acc_ref)
    carries = (jnp.zeros((1, blk), F32),) * nh
    lead = nh // 2

    def one_step(kb_w, kb_s, diagonal, carries, last=False, between=None):
        for h in range(nh):
            if between is not None:
                between()
            ahead = h + lead
            if ahead < nh:
                mx = (ahead, kb_s, kb_w is not None)
            elif last:
                mx = None
            else:
                mx = (ahead - nh, jnp.maximum(kb_s - 1, 0), True)
            carries = part(ew=(h, kb_w, kb_s is not None, diagonal, carries), mx=mx)
        return carries

    @pl.when((pl.program_id(1) == 0) & (qi == 0))
    def _():
        state_ref[...] = jnp.zeros_like(state_ref)

    rwkv = _rwkv_stages(rt_ref, at_ref, bt_ref, kt_ref, v_ref, bh_ref, kh_ref, wc_ref, y_ref,
                        state_ref)

    def rwkv_advance(n):
        for _ in range(n):
            next(rwkv, None)

    for h in range(lead):
        part(mx=(h, qi, False))
        rwkv_advance(1)
    carries = one_step(None, qi, True, carries, between=lambda: rwkv_advance(1))
    odd = qi % 2
    carries = lax.cond(odd == 1, lambda cs: one_step(qi, qi - 1, False, cs), lambda cs: cs,
                       carries)

    def two_steps(j, cs):
        kb_w = qi - odd - 2 * j
        cs = one_step(kb_w, kb_w - 1, False, cs)
        return one_step(kb_w - 1, kb_w - 2, False, cs)

    carries = lax.fori_loop(0, qi // 2, two_steps, carries)
    one_step(0, None, False, carries, last=True, between=lambda: rwkv_advance(2))
    for _ in rwkv:
        pass

    parts = []
    for h in range(nh):
        a = acc_ref[h]
        ms = jnp.mean(a * a, axis=0, keepdims=True)
        parts.append(a * lax.rsqrt(ms + NORM_EPS))
    out = jnp.concatenate(parts, axis=0).T
    o_ref[...] = (out * g_ref[...]).astype(BF16)


def _mixer(q, k, vt, sb_g, rt, at, bt, kt, v, bh, kh, wc, blk):
    t = q.shape[0]
    b, _, s = vt.shape
    w = RWKV_WIDTH
    ngroup = SB_WIDTH // GROUP
    nq = s // blk
    nh = GROUP // HEAD_DIM
    span = s // (ngroup * nq)
    assert span == 2 * CHUNK, (s, blk)
    seq = lambda x: x.reshape(b, s, w)
    tile = pltpu.VMEM((1, blk, blk), BF16)
    row = pltpu.VMEM((1, 1, blk), F32)
    rw_tile = pl.BlockSpec((1, span, w), lambda bi, g, qi: (bi, g * nq + qi, 0))
    return pl.pallas_call(
        functools.partial(_mixer_body, blk=blk),
        grid=(b, ngroup, nq),
        in_specs=[
            pl.BlockSpec((blk, GROUP), lambda bi, g, qi: (bi * nq + qi, g)),
            pl.BlockSpec((s, GROUP), lambda bi, g, qi: (bi, g)),
            pl.BlockSpec((1, GROUP, s), lambda bi, g, qi: (bi, g, 0)),
            pl.BlockSpec((1, GROUP), lambda bi, g, qi: (0, g)),
        ] + [rw_tile] * 7 + [
            pl.BlockSpec((1, span // 8, w), lambda bi, g, qi: (bi, g * nq + qi, 0)),
        ],
        out_specs=[
            pl.BlockSpec((blk, GROUP), lambda bi, g, qi: (bi * nq + qi, g)),
            rw_tile,
        ],
        out_shape=[
            jax.ShapeDtypeStruct((t, SB_WIDTH), BF16),
            jax.ShapeDtypeStruct((b, s, w), BF16),
        ],
        scratch_shapes=[
            pltpu.VMEM((nh, HEAD_DIM, blk), F32),
            pltpu.VMEM((nh, blk, GROUP), BF16),
            pltpu.VMEM((w // PAIR, PAIR, PAIR), F32),
        ] + [tile, tile, tile, tile, row] * nh,
        compiler_params=pltpu.CompilerParams(
            dimension_semantics=("arbitrary", "arbitrary", "arbitrary"),
            vmem_limit_bytes=VMEM_LIMIT),
        name="mixer",
    )(q, k, vt, sb_g, seq(rt), seq(at), seq(bt), seq(kt), seq(v), seq(bh), seq(kh),
      wc.reshape(b, s // 8, w))


def _rwkv_stages(rt_ref, at_ref, bt_ref, kt_ref, v_ref, bh_ref, kh_ref, wc_ref, y_ref, state_ref):
    w = RWKV_WIDTH
    npair = w // PAIR
    nchunk = rt_ref.shape[1] // CHUNK
    njob = nchunk * npair
    n2 = 2 * CHUNK
    lane = lax.broadcasted_iota(jnp.int32, (CHUNK, PAIR), 1)
    first = lane < HEAD_DIM
    br = lax.broadcasted_iota(jnp.int32, (n2, n2), 0)
    bc = lax.broadcasted_iota(jnp.int32, (n2, n2), 1)
    same_head = (br // CHUNK) == (bc // CHUNK)
    strict = same_head & (bc < br)
    incl = same_head & (bc <= br)
    eye = br == bc

    def stack_masked(x):
        return jnp.concatenate([jnp.where(first, x, 0.0), jnp.where(first, 0.0, x)], axis=0)

    def stack(x):
        return jnp.concatenate([x, x], axis=0)

    sls = [(0, slice((jb // npair) * CHUNK, (jb // npair + 1) * CHUNK),
            slice((jb % npair) * PAIR, (jb % npair + 1) * PAIR)) for jb in range(njob)]
    each = lambda f: [f(jb) for jb in range(njob)]
    a_sm = each(lambda pi: stack_masked(at_ref[sls[pi]]))
    r_sm = each(lambda pi: stack_masked(rt_ref[sls[pi]]))
    v_sm = each(lambda pi: stack_masked(v_ref[sls[pi]]))
    bh_sm = each(lambda pi: stack_masked(bh_ref[sls[pi]]))
    kh_sm = each(lambda pi: stack_masked(kh_ref[sls[pi]]))
    wc = each(lambda pi: wc_ref[0, (pi // npair) * 8:(pi // npair) * 8 + 1, sls[pi][2]])
    mm = each(lambda pi: _dot(
        jnp.concatenate([a_sm[pi], r_sm[pi]], axis=0),
        jnp.concatenate([stack(bt_ref[sls[pi]]), stack(kt_ref[sls[pi]])], axis=0),
        _NT))
    yield
    m_ab = each(lambda pi: jnp.where(strict, mm[pi][:n2, :n2], 0.0))
    m_ak = each(lambda pi: jnp.where(strict, mm[pi][:n2, n2:], 0.0).astype(BF16))
    m_rbk = each(lambda pi: jnp.concatenate(
        [jnp.where(incl, mm[pi][n2:, :n2], 0.0), jnp.where(incl, mm[pi][n2:, n2:], 0.0)],
        axis=1).astype(BF16))
    rhs = each(lambda pi: jnp.concatenate(
        [a_sm[pi], _dot(m_ak[pi], v_sm[pi]).astype(BF16)], axis=1))
    inv = each(lambda pi: jnp.where(eye, 1.0, m_ab[pi]))
    lp16 = each(lambda pi: m_ab[pi].astype(BF16))
    lp16 = each(lambda pi: _dot(lp16[pi], lp16[pi]).astype(BF16))
    yield
    steps = CHUNK.bit_length() - 1
    for i in range(1, steps):
        if i < steps - 1:
            both = each(lambda pi: _dot(
                lp16[pi], jnp.concatenate([lp16[pi], inv[pi].astype(BF16)], axis=1)))
            lp16 = each(lambda pi: both[pi][:, :n2].astype(BF16))
            inv = each(lambda pi: inv[pi] + both[pi][:, n2:])
        else:
            inv = each(lambda pi: inv[pi] + _dot(lp16[pi], inv[pi].astype(BF16)))
        yield
    pq = each(lambda pi: _dot(inv[pi].astype(BF16), rhs[pi]).astype(BF16))
    yield
    zeros = jnp.zeros((n2, PAIR), BF16)
    gy = each(lambda pi: _dot(m_rbk[pi], jnp.concatenate(
        [pq[pi], jnp.concatenate([zeros, v_sm[pi]], axis=1)], axis=0)))
    phi = each(lambda pi: _dot(pq[pi][:, :PAIR], bh_sm[pi], _TN) + jnp.where(eye, wc[pi], 0.0))
    psi = each(lambda pi: _dot(jnp.concatenate([pq[pi][:, PAIR:], v_sm[pi]], axis=0),
                               jnp.concatenate([bh_sm[pi], kh_sm[pi]], axis=0), _TN))
    yield
    g_pk = each(lambda pi: (r_sm[pi][:CHUNK] + gy[pi][:CHUNK, :PAIR]
                            + r_sm[pi][CHUNK:] + gy[pi][CHUNK:, :PAIR]).astype(BF16))
    y0 = each(lambda pi: gy[pi][:CHUNK, PAIR:] + gy[pi][CHUNK:, PAIR:])
    phi16 = each(lambda pi: phi[pi].astype(BF16))
    state = [state_ref[pi] for pi in range(npair)]
    ys = [None] * njob
    for ci in range(nchunk):
        s16 = [st.astype(BF16) for st in state]
        for pi in range(npair):
            jb = ci * npair + pi
            ys[jb] = _dot(g_pk[jb], s16[pi], _NT) + y0[jb]
        state = [_dot(s16[pi], phi16[ci * npair + pi]) + psi[ci * npair + pi]
                 for pi in range(npair)]
        yield
    for jb in range(njob):
        y_ref[sls[jb]] = ys[jb].astype(BF16)
    for pi in range(npair):
        state_ref[pi] = state[pi]


def _outproj_ffn_body(sb_ref, y_ref, bonus_ref, gate_ref, gng_ref, gnb_ref, x_ref, wa_ref, wb_ref,
                      mg_ref, pre_ref, post_ref, wg_ref, wu_ref, wd_ref, o_ref):
    y = y_ref[...].astype(F32)
    inv = 1.0 / HEAD_DIM
    d = y - _head_sums(y) * inv
    var = _head_sums(d * d) * inv
    yn = d * lax.rsqrt(var + GN_EPS) * gng_ref[...] + gnb_ref[...]
    yg = ((yn + bonus_ref[...].astype(F32)) * gate_ref[...].astype(F32)).astype(BF16)
    m = _dot(sb_ref[...], wa_ref[...]) + _dot(yg, wb_ref[...])
    x = x_ref[...] + _rms(m, mg_ref[...])
    xn = _rms(x, pre_ref[...]).astype(BF16)
    g = _dot(xn, wg_ref[...])
    u = _dot(xn, wu_ref[...])
    h = (g * _sigmoid(g) * u).astype(BF16)
    f = _dot(h, wd_ref[...])
    o_ref[...] = x + 0.5 * _rms(f, post_ref[...])


def _outproj_ffn(sb2d, y2d, bonus, gate, gn_g, gn_b, x2d, w_out, mix_g, pre_g, post_g,
                 wg, wu, wd, tm):
    t, d = x2d.shape
    dff = wg.shape[1]
    const = lambda i: (0, 0)
    row = lambda i: (i, 0)
    once = dict(pipeline_mode=pl.Buffered(1))
    return pl.pallas_call(
        _outproj_ffn_body,
        grid=(t // tm,),
        in_specs=[
            pl.BlockSpec((tm, SB_WIDTH), row),
            pl.BlockSpec((tm, RWKV_WIDTH), row),
            pl.BlockSpec((tm, RWKV_WIDTH), row),
            pl.BlockSpec((tm, RWKV_WIDTH), row),
            pl.BlockSpec((1, RWKV_WIDTH), const),
            pl.BlockSpec((1, RWKV_WIDTH), const),
            pl.BlockSpec((tm, d), row),
            pl.BlockSpec((SB_WIDTH, d), lambda i: (0, 0), **once),
            pl.BlockSpec((RWKV_WIDTH, d), lambda i: (1, 0), **once),
            pl.BlockSpec((1, d), const),
            pl.BlockSpec((1, d), const),
            pl.BlockSpec((1, d), const),
            pl.BlockSpec((d, dff), const, **once),
            pl.BlockSpec((d, dff), const, **once),
            pl.BlockSpec((dff, d), const, **once),
        ],
        out_specs=pl.BlockSpec((tm, d), row),
        out_shape=jax.ShapeDtypeStruct((t, d), F32),
        compiler_params=pltpu.CompilerParams(
            dimension_semantics=("arbitrary",), vmem_limit_bytes=VMEM_LIMIT),
        name="outproj_ffn",
    )(sb2d, y2d, bonus, gate, gn_g, gn_b, x2d, w_out, w_out, mix_g, pre_g, post_g, wg, wu, wd)


def _pick(n, pref):
    return pref if n % pref == 0 else n


def kernel(x, ffn1_pre_g, ffn1_post_g, ffn1_w_gate, ffn1_w_up, ffn1_w_down, mix_pre_g, mix_post_g, w_in, shift_mu, sb_out_g, decay_w0, decay_w2, iclr_a0, iclr_a2, gate_w2, k_k, k_a, r_k, gn_g, gn_b, w_out, ffn2_pre_g, ffn2_post_g, ffn2_w_gate, ffn2_w_up, ffn2_w_down):
    b, s, d = x.shape
    depth = ffn1_pre_g.shape[0]
    t = b * s
    tm = _pick(s, 512)
    blk = _pick(s, 256)
    x2d = x.reshape(t, d)
    for l in range(depth):
        x2d = _ffn(x2d, ffn1_pre_g[l][None], ffn1_post_g[l][None], ffn1_w_gate[l].astype(BF16),
                   ffn1_w_up[l].astype(BF16), ffn1_w_down[l].astype(BF16), tm)
        zpad = jnp.zeros((AAA_LORA, RWKV_WIDTH), F32)
        dw2 = jnp.concatenate([decay_w2[l], zpad], axis=0).astype(BF16)
        aw2 = jnp.concatenate([zpad, iclr_a2[l]], axis=0).astype(BF16)
        q, k, vt, rt, at, bt, kt, v, bh, kh, wc, bonus, gate = _inproj(
            x2d, mix_pre_g[l][None], w_in[l].astype(BF16), shift_mu[l][None], decay_w0[l][None],
            dw2, iclr_a0[l][None], aw2, gate_w2[l].astype(BF16), k_k[l][None], k_a[l][None],
            r_k[l].reshape(1, -1), tm, s)
        sb, y = _mixer(q, k, vt, sb_out_g[l][None], rt, at, bt, kt, v, bh, kh, wc, blk)
        x2d = _outproj_ffn(sb, y.reshape(t, -1), bonus, gate, gn_g[l][None], gn_b[l][None], x2d,
                           w_out[l].astype(BF16), mix_post_g[l][None],
                           ffn2_pre_g[l][None], ffn2_post_g[l][None], ffn2_w_gate[l].astype(BF16),
                           ffn2_w_up[l].astype(BF16), ffn2_w_down[l].astype(BF16), tm)
    return x2d.reshape(b, s, d)
```

```python
import functools

import jax
import jax.numpy as jnp
from jax import lax
from jax.experimental import pallas as pl
from jax.experimental.pallas import tpu as pltpu

F32 = jnp.float32
BF16 = jnp.bfloat16

HEAD_DIM = 64
SB_HEADS = 8
RWKV_HEADS = 8
SB_WIDTH = SB_HEADS * HEAD_DIM
RWKV_WIDTH = RWKV_HEADS * HEAD_DIM
DECAY_LORA = 64
AAA_LORA = 64
GATE_LORA = 128
NORM_EPS = 1e-6
GN_EPS = HEAD_DIM * 1e-5

LANES = 128
PAIR = LANES
GROUP = 2 * LANES
LOG2E = 1.4426950408889634
LOG2E_BF16 = (1.4453125, -0.00262451171875, 7.033348083496094e-06)
CHUNK = 64
VMEM_LIMIT = 56 * 1024 * 1024

_NT = (((1,), (1,)), ((), ()))
_TN = (((0,), (0,)), ((), ()))


def _dot(a, b, dims=None):
    if dims is None:
        dims = (((a.ndim - 1,), (0,)), ((), ()))
    return lax.dot_general(a, b, dims, preferred_element_type=F32)


def _rms(x, g):
    return x * lax.rsqrt(jnp.mean(x * x, axis=-1, keepdims=True) + NORM_EPS) * g


def _softplus(x):
    return jnp.maximum(x, 0.0) + jnp.log(1.0 + jnp.exp(-jnp.abs(x)))


def _sigmoid(x):
    return 1.0 / (1.0 + jnp.exp(-x))


def _split(x):
    hi = x.astype(BF16)
    return hi, (x - hi.astype(F32)).astype(BF16)


def _head_sums(x):
    r = lax.broadcasted_iota(jnp.int32, (GROUP, GROUP), 0) // HEAD_DIM
    c = lax.broadcasted_iota(jnp.int32, (GROUP, GROUP), 1) // HEAD_DIM
    ones = (r == c).astype(BF16)
    parts = [_dot(x[:, gi * GROUP:(gi + 1) * GROUP].astype(BF16), ones)
             for gi in range(x.shape[1] // GROUP)]
    return jnp.concatenate(parts, axis=1)


def _ffn_body(x_ref, pre_ref, post_ref, wg_ref, wu_ref, wd_ref, o_ref):
    x = x_ref[...]
    xn = _rms(x, pre_ref[...]).astype(BF16)
    g = _dot(xn, wg_ref[...])
    u = _dot(xn, wu_ref[...])
    h = (g * _sigmoid(g) * u).astype(BF16)
    f = _dot(h, wd_ref[...])
    o_ref[...] = x + 0.5 * _rms(f, post_ref[...])


def _ffn(x2d, pre_g, post_g, wg, wu, wd, tm):
    t, d = x2d.shape
    dff = wg.shape[1]
    const = lambda i: (0, 0)
    return pl.pallas_call(
        _ffn_body,
        grid=(t // tm,),
        in_specs=[
            pl.BlockSpec((tm, d), lambda i: (i, 0)),
            pl.BlockSpec((1, d), const),
            pl.BlockSpec((1, d), const),
            pl.BlockSpec((d, dff), const, pipeline_mode=pl.Buffered(1)),
            pl.BlockSpec((d, dff), const, pipeline_mode=pl.Buffered(1)),
            pl.BlockSpec((dff, d), const, pipeline_mode=pl.Buffered(1)),
        ],
        out_specs=pl.BlockSpec((tm, d), lambda i: (i, 0)),
        out_shape=jax.ShapeDtypeStruct((t, d), F32),
        compiler_params=pltpu.CompilerParams(
            dimension_semantics=("arbitrary",), vmem_limit_bytes=VMEM_LIMIT),
        name="ffn",
    )(x2d, pre_g, post_g, wg, wu, wd)


def _inproj_body(x_ref, g_ref, w_ref, mu_ref, w0_ref, dw2_ref, a0_ref, aw2_ref, gw2_ref,
                 kk_ref, ka_ref, rk_ref,
                 q_ref, k_ref, vt_ref, rt_ref, at_ref, bt_ref, kt_ref, v_ref, bh_ref, kh_ref,
                 wc_ref, bonus_ref, gate_ref, last_ref, *, per_seq):
    i = pl.program_id(0)
    w = RWKV_WIDTH
    nsb = 3 * SB_WIDTH
    tm = x_ref.shape[0]

    @pl.when(i % per_seq == 0)
    def _():
        last_ref[...] = jnp.zeros_like(last_ref)

    xn = _rms(x_ref[...], g_ref[...]).astype(BF16)
    p = _dot(xn, w_ref[:, nsb:])
    psb = _dot(xn, w_ref[:, :nsb])
    q_ref[...] = (psb[:, :SB_WIDTH] * (HEAD_DIM ** -0.5 * LOG2E)).astype(BF16)
    k_ref[...] = psb[:, SB_WIDTH:2 * SB_WIDTH].astype(BF16)
    vt_ref[0] = psb[:, 2 * SB_WIDTH:].T.astype(BF16)

    rows = lax.broadcasted_iota(jnp.int32, p.shape, 0)
    prev = jnp.where(rows == 0, last_ref[...], pltpu.roll(p, 1, 0))
    last_ref[...] = p[tm - 1:tm, :]
    xs = p + (prev - p) * mu_ref[...]
    r = xs[:, 0:w]
    kr = xs[:, w:2 * w]
    vr = xs[:, 2 * w:3 * w]
    lora_in = xs[:, 3 * w:3 * w + DECAY_LORA + AAA_LORA]
    gd = xs[:, 3 * w + DECAY_LORA + AAA_LORA:]
    lora_w = _dot(jnp.tanh(lora_in).astype(BF16), dw2_ref[...])
    lora_a = _dot(lora_in.astype(BF16), aw2_ref[...])
    gate_ref[...] = _dot(_sigmoid(gd).astype(BF16), gw2_ref[...]).astype(BF16)
    w_log = -_softplus(-(w0_ref[...] + lora_w)) - 0.5
    lw2 = jnp.exp(w_log) * (-LOG2E)
    a = _sigmoid(a0_ref[...] + lora_a)
    kk = kr * kk_ref[...]
    k2 = kr * (1.0 + (a - 1.0) * ka_ref[...])
    bonus_ref[...] = (_head_sums(r * k2 * rk_ref[...]) * vr).astype(BF16)
    kkn = kk * lax.rsqrt(jnp.maximum(_head_sums(kk * kk), 1e-24))
    a_vec = -kkn
    b_vec = kkn * a
    v_ref[...] = vr.astype(BF16)

    sub = 4 * CHUNK
    tr = lax.broadcasted_iota(jnp.int32, (sub, sub), 0)
    tc = lax.broadcasted_iota(jnp.int32, (sub, sub), 1)
    prefix = (((tr // CHUNK) == (tc // CHUNK)) & (tc <= tr)).astype(BF16)
    hi, lo = _split(lw2)
    cums, tots, ends = [], [], []
    for s0 in range(0, tm, sub):
        cum = _dot(prefix, hi[s0:s0 + sub]) + _dot(prefix, lo[s0:s0 + sub])
        cums.append(cum)
        for c in range(sub // CHUNK):
            end = cum[c * CHUNK + CHUNK - 1:(c + 1) * CHUNK, :]
            ends.append(jnp.broadcast_to(end, (8, w)))
            tots.append(jnp.broadcast_to(end, (CHUNK, w)))
    cum = jnp.concatenate(cums, axis=0)
    tot = jnp.concatenate(tots, axis=0)
    dec_out = jnp.exp2(-cum)
    dec_rest = jnp.exp2(tot - cum)
    rt_ref[...] = (r * jnp.exp2(cum)).astype(BF16)
    at_ref[...] = (a_vec * jnp.exp2(cum - lw2)).astype(BF16)
    bt_ref[...] = (b_vec * dec_out).astype(BF16)
    kt_ref[...] = (k2 * dec_out).astype(BF16)
    bh_ref[...] = (b_vec * dec_rest).astype(BF16)
    kh_ref[...] = (k2 * dec_rest).astype(BF16)
    wc_ref[...] = jnp.exp2(jnp.concatenate(ends, axis=0))


def _inproj(x2d, g, w_in, mu, w0, dw2, a0, aw2, gw2, k_k, k_a, r_k, tm, seq):
    t, d = x2d.shape
    n = w_in.shape[1]
    w = RWKV_WIDTH
    per_seq = seq // tm
    const = lambda i: (0, 0)
    row = lambda i: (i, 0)
    vec = lambda m: pl.BlockSpec((1, m), const)
    tile = pl.BlockSpec((tm, w), row)
    wide = jax.ShapeDtypeStruct((t, w), BF16)
    return pl.pallas_call(
        functools.partial(_inproj_body, per_seq=per_seq),
        grid=(t // tm,),
        in_specs=[
            pl.BlockSpec((tm, d), row),
            vec(d),
            pl.BlockSpec((d, n), const, pipeline_mode=pl.Buffered(1)),
            vec(n - 3 * SB_WIDTH), vec(w),
            pl.BlockSpec((DECAY_LORA + AAA_LORA, w), const),
            vec(w),
            pl.BlockSpec((DECAY_LORA + AAA_LORA, w), const),
            pl.BlockSpec((GATE_LORA, w), const),
            vec(w), vec(w), vec(w),
        ],
        out_specs=[
            pl.BlockSpec((tm, SB_WIDTH), row),
            pl.BlockSpec((tm, SB_WIDTH), row),
            pl.BlockSpec((1, SB_WIDTH, tm), lambda i: (i // per_seq, 0, i % per_seq)),
            tile, tile, tile, tile, tile, tile, tile,
            pl.BlockSpec((tm // 8, w), row),
            tile, tile,
        ],
        out_shape=[
            jax.ShapeDtypeStruct((t, SB_WIDTH), BF16),
            jax.ShapeDtypeStruct((t, SB_WIDTH), BF16),
            jax.ShapeDtypeStruct((t // seq, SB_WIDTH, seq), BF16),
            wide, wide, wide, wide, wide, wide, wide,
            jax.ShapeDtypeStruct((t // 8, w), F32),
            wide, wide,
        ],
        scratch_shapes=[pltpu.VMEM((1, n - 3 * SB_WIDTH), F32)],
        compiler_params=pltpu.CompilerParams(
            dimension_semantics=("arbitrary",), vmem_limit_bytes=VMEM_LIMIT),
        name="inproj",
    )(x2d, g, w_in, mu, w0, dw2, a0, aw2, gw2, k_k, k_a, r_k)


def _mixer_body(q_ref, k_ref, vt_ref, g_ref, rt_ref, at_ref, bt_ref, kt_ref, v_ref, bh_ref,
                kh_ref, wc_ref, o_ref, y_ref, acc_ref, qh_ref, state_ref, *stage_refs, blk):
    qi = pl.program_id(2)
    nh = GROUP // HEAD_DIM
    halves = tuple((h,) for h in range(nh))
    names = ("zraw", "craw", "z", "sp", "tot")
    bufs = tuple(dict(zip(names, stage_refs[len(names) * h:len(names) * (h + 1)]))
                 for h in range(nh))

    q = q_ref[...]
    lane = lax.broadcasted_iota(jnp.int32, (blk, GROUP), 1)
    zero = jnp.zeros_like(q)
    for h in range(nh):
        qh_ref[h] = jnp.where(lane // HEAD_DIM == h, q, zero)
    key = lax.broadcasted_iota(jnp.int32, (blk, blk), 0)
    qry = lax.broadcasted_iota(jnp.int32, (blk, blk), 1)
    neg_after = jnp.where(qry > key, -1.0, 0.0).astype(BF16)
    causal = key < qry

    def rows_of(kb):
        return pl.ds(pl.multiple_of(kb * blk, blk), blk)

    def part(ew=None, mx=None):
        if mx is not None:
            mhalf, kb_s, do_sums = mx
            mb = bufs[mhalf]
            if do_sums:
                sums = [_dot(neg_after, mb["sp"][i]) for i in range(len(halves[mhalf]))]
            if kb_s is not None:
                kblk = k_ref[rows_of(kb_s), :]
                raw = [_dot(kblk, qh_ref[h], _NT) for h in halves[mhalf]]
        new = None
        if ew is not None:
            ehalf, kb_w, do_score, diagonal, carries = ew
            eb = bufs[ehalf]
            new = list(carries)
            outs = []
            if kb_w is not None:
                for i, h in enumerate(halves[ehalf]):
                    new[h] = eb["tot"][i] + carries[h]
                    w = jnp.exp2(eb["z"][i] + eb["craw"][i])
                    vt = vt_ref[0, h * HEAD_DIM:(h + 1) * HEAD_DIM, rows_of(kb_w)]
                    outs.append(_dot(vt, w) * jnp.exp2(carries[h]))
            if do_score:
                for i in range(len(halves[ehalf])):
                    z = eb["zraw"][i]
                    if diagonal:
                        z = jnp.where(causal, z, -1e30)
                    m = jnp.maximum(z, 0.0)
                    n = jnp.minimum(z, 0.0)
                    ln = jnp.log(1.0 + jnp.exp2(n - m))
                    soft = ln * LOG2E_BF16[0] + (ln * LOG2E_BF16[1] + ln * LOG2E_BF16[2])
                    eb["z"][i] = n - soft
                    eb["sp"][i] = m + soft
        if mx is not None:
            for i in range(len(halves[mhalf])):
                if do_sums:
                    mb["tot"][i] = sums[i][0:1, :] - mb["sp"][i][0:1, :].astype(F32)
                    mb["craw"][i] = sums[i].astype(BF16)
                if kb_s is not None:
                    mb["zraw"][i] = raw[i].astype(BF16)
        if ew is not None and kb_w is not None:
            for i, h in enumerate(halves[ehalf]):
                acc_ref[h] += outs[i]
        return None if new is None else tuple(new)

    acc_ref[...] = jnp.zeros_like(acc_ref)
    carries = (jnp.zeros((1, blk), F32),) * nh
    lead = nh // 2

    def one_step(kb_w, kb_s, diagonal, carries, last=False, between=None):
        for h in range(nh):
            if between is not None:
                between()
            ahead = h + lead
            if ahead < nh:
                mx = (ahead, kb_s, kb_w is not None)
            elif last:
                mx = None
            else:
                mx = (ahead - nh, jnp.maximum(kb_s - 1, 0), True)
            carries = part(ew=(h, kb_w, kb_s is not None, diagonal, carries), mx=mx)
        return carries

    @pl.when((pl.program_id(1) == 0) & (qi == 0))
    def _():
        state_ref[...] = jnp.zeros_like(state_ref)

    rwkv = _rwkv_stages(rt_ref, at_ref, bt_ref, kt_ref, v_ref, bh_ref, kh_ref, wc_ref, y_ref,
                        state_ref)

    def rwkv_advance(n):
        for _ in range(n):
            next(rwkv, None)

    for h in range(lead):
        part(mx=(h, qi, False))
        rwkv_advance(1)
    carries = one_step(None, qi, True, carries, between=lambda: rwkv_advance(1))
    odd = qi % 2
    carries = lax.cond(odd == 1, lambda cs: one_step(qi, qi - 1, False, cs), lambda cs: cs,
                       carries)

    def two_steps(j, cs):
        kb_w = qi - odd - 2 * j
        cs = one_step(kb_w, kb_w - 1, False, cs)
        return one_step(kb_w - 1, kb_w - 2, False, cs)

    carries = lax.fori_loop(0, qi // 2, two_steps, carries)
    one_step(0, None, False, carries, last=True, between=lambda: rwkv_advance(2))
    for _ in rwkv:
        pass

    parts = []
    for h in range(nh):
        a = acc_ref[h]
        ms = jnp.mean(a * a, axis=0, keepdims=True)
        parts.append(a * lax.rsqrt(ms + NORM_EPS))
    out = jnp.concatenate(parts, axis=0).T
    o_ref[...] = (out * g_ref[...]).astype(BF16)


def _mixer(q, k, vt, sb_g, rt, at, bt, kt, v, bh, kh, wc, blk):
    t = q.shape[0]
    b, _, s = vt.shape
    w = RWKV_WIDTH
    ngroup = SB_WIDTH // GROUP
    nq = s // blk
    nh = GROUP // HEAD_DIM
    span = s // (ngroup * nq)
    assert span == 2 * CHUNK, (s, blk)
    seq = lambda x: x.reshape(b, s, w)
    tile = pltpu.VMEM((1, blk, blk), BF16)
    row = pltpu.VMEM((1, 1, blk), F32)
    rw_tile = pl.BlockSpec((1, span, w), lambda bi, g, qi: (bi, g * nq + qi, 0))
    return pl.pallas_call(
        functools.partial(_mixer_body, blk=blk),
        grid=(b, ngroup, nq),
        in_specs=[
            pl.BlockSpec((blk, GROUP), lambda bi, g, qi: (bi * nq + qi, g)),
            pl.BlockSpec((s, GROUP), lambda bi, g, qi: (bi, g)),
            pl.BlockSpec((1, GROUP, s), lambda bi, g, qi: (bi, g, 0)),
            pl.BlockSpec((1, GROUP), lambda bi, g, qi: (0, g)),
        ] + [rw_tile] * 7 + [
            pl.BlockSpec((1, span // 8, w), lambda bi, g, qi: (bi, g * nq + qi, 0)),
        ],
        out_specs=[
            pl.BlockSpec((blk, GROUP), lambda bi, g, qi: (bi * nq + qi, g)),
            rw_tile,
        ],
        out_shape=[
            jax.ShapeDtypeStruct((t, SB_WIDTH), BF16),
            jax.ShapeDtypeStruct((b, s, w), BF16),
        ],
        scratch_shapes=[
            pltpu.VMEM((nh, HEAD_DIM, blk), F32),
            pltpu.VMEM((nh, blk, GROUP), BF16),
            pltpu.VMEM((w // PAIR, PAIR, PAIR), F32),
        ] + [tile, tile, tile, tile, row] * nh,
        compiler_params=pltpu.CompilerParams(
            dimension_semantics=("arbitrary", "arbitrary", "arbitrary"),
            vmem_limit_bytes=VMEM_LIMIT),
        name="mixer",
    )(q, k, vt, sb_g, seq(rt), seq(at), seq(bt), seq(kt), seq(v), seq(bh), seq(kh),
      wc.reshape(b, s // 8, w))


def _rwkv_stages(rt_ref, at_ref, bt_ref, kt_ref, v_ref, bh_ref, kh_ref, wc_ref, y_ref, state_ref):
    w = RWKV_WIDTH
    npair = w // PAIR
    nchunk = rt_ref.shape[1] // CHUNK
    njob = nchunk * npair
    n2 = 2 * CHUNK
    lane = lax.broadcasted_iota(jnp.int32, (CHUNK, PAIR), 1)
    first = lane < HEAD_DIM
    br = lax.broadcasted_iota(jnp.int32, (n2, n2), 0)
    bc = lax.broadcasted_iota(jnp.int32, (n2, n2), 1)
    same_head = (br // CHUNK) == (bc // CHUNK)
    strict = same_head & (bc < br)
    incl = same_head & (bc <= br)
    eye = br == bc

    def stack_masked(x):
        return jnp.concatenate([jnp.where(first, x, 0.0), jnp.where(first, 0.0, x)], axis=0)

    def stack(x):
        return jnp.concatenate([x, x], axis=0)

    sls = [(0, slice((jb // npair) * CHUNK, (jb // npair + 1) * CHUNK),
            slice((jb % npair) * PAIR, (jb % npair + 1) * PAIR)) for jb in range(njob)]
    each = lambda f: [f(jb) for jb in range(njob)]
    a_sm = each(lambda pi: stack_masked(at_ref[sls[pi]]))
    r_sm = each(lambda pi: stack_masked(rt_ref[sls[pi]]))
    v_sm = each(lambda pi: stack_masked(v_ref[sls[pi]]))
    bh_sm = each(lambda pi: stack_masked(bh_ref[sls[pi]]))
    kh_sm = each(lambda pi: stack_masked(kh_ref[sls[pi]]))
    wc = each(lambda pi: wc_ref[0, (pi // npair) * 8:(pi // npair) * 8 + 1, sls[pi][2]])
    mm = each(lambda pi: _dot(
        jnp.concatenate([a_sm[pi], r_sm[pi]], axis=0),
        jnp.concatenate([stack(bt_ref[sls[pi]]), stack(kt_ref[sls[pi]])], axis=0),
        _NT))
    yield
    m_ab = each(lambda pi: jnp.where(strict, mm[pi][:n2, :n2], 0.0))
    m_ak = each(lambda pi: jnp.where(strict, mm[pi][:n2, n2:], 0.0).astype(BF16))
    m_rbk = each(lambda pi: jnp.concatenate(
        [jnp.where(incl, mm[pi][n2:, :n2], 0.0), jnp.where(incl, mm[pi][n2:, n2:], 0.0)],
        axis=1).astype(BF16))
    rhs = each(lambda pi: jnp.concatenate(
        [a_sm[pi], _dot(m_ak[pi], v_sm[pi]).astype(BF16)], axis=1))
    inv = each(lambda pi: jnp.where(eye, 1.0, m_ab[pi]))
    lp16 = each(lambda pi: m_ab[pi].astype(BF16))
    lp16 = each(lambda pi: _dot(lp16[pi], lp16[pi]).astype(BF16))
    yield
    steps = CHUNK.bit_length() - 1
    for i in range(1, steps):
        if i < steps - 1:
            both = each(lambda pi: _dot(
                lp16[pi], jnp.concatenate([lp16[pi], inv[pi].astype(BF16)], axis=1)))
            lp16 = each(lambda pi: both[pi][:, :n2].astype(BF16))
            inv = each(lambda pi: inv[pi] + both[pi][:, n2:])
        else:
            inv = each(lambda pi: inv[pi] + _dot(lp16[pi], inv[pi].astype(BF16)))
        yield
    pq = each(lambda pi: _dot(inv[pi].astype(BF16), rhs[pi]).astype(BF16))
    yield
    zeros = jnp.zeros((n2, PAIR), BF16)
    gy = each(lambda pi: _dot(m_rbk[pi], jnp.concatenate(
        [pq[pi], jnp.concatenate([zeros, v_sm[pi]], axis=1)], axis=0)))
    phi = each(lambda pi: _dot(pq[pi][:, :PAIR], bh_sm[pi], _TN) + jnp.where(eye, wc[pi], 0.0))
    psi = each(lambda pi: _dot(jnp.concatenate([pq[pi][:, PAIR:], v_sm[pi]], axis=0),
                               jnp.concatenate([bh_sm[pi], kh_sm[pi]], axis=0), _TN))
    yield
    g_pk = each(lambda pi: (r_sm[pi][:CHUNK] + gy[pi][:CHUNK, :PAIR]
                            + r_sm[pi][CHUNK:] + gy[pi][CHUNK:, :PAIR]).astype(BF16))
    y0 = each(lambda pi: gy[pi][:CHUNK, PAIR:] + gy[pi][CHUNK:, PAIR:])
    phi16 = each(lambda pi: phi[pi].astype(BF16))
    state = [state_ref[pi] for pi in range(npair)]
    ys = [None] * njob
    for ci in range(nchunk):
        s16 = [st.astype(BF16) for st in state]
        for pi in range(npair):
            jb = ci * npair + pi
            ys[jb] = _dot(g_pk[jb], s16[pi], _NT) + y0[jb]
        state = [_dot(s16[pi], phi16[ci * npair + pi]) + psi[ci * npair + pi]
                 for pi in range(npair)]
        yield
    for jb in range(njob):
        y_ref[sls[jb]] = ys[jb].astype(BF16)
    for pi in range(npair):
        state_ref[pi] = state[pi]


def _outproj_ffn_body(sb_ref, y_ref, bonus_ref, gate_ref, gng_ref, gnb_ref, x_ref, wa_ref, wb_ref,
                      mg_ref, pre_ref, post_ref, wg_ref, wu_ref, wd_ref, o_ref):
    y = y_ref[...].astype(F32)
    inv = 1.0 / HEAD_DIM
    d = y - _head_sums(y) * inv
    var = _head_sums(d * d) * inv
    yn = d * lax.rsqrt(var + GN_EPS) * gng_ref[...] + gnb_ref[...]
    yg = ((yn + bonus_ref[...].astype(F32)) * gate_ref[...].astype(F32)).astype(BF16)
    m = _dot(sb_ref[...], wa_ref[...]) + _dot(yg, wb_ref[...])
    x = x_ref[...] + _rms(m, mg_ref[...])
    xn = _rms(x, pre_ref[...]).astype(BF16)
    g = _dot(xn, wg_ref[...])
    u = _dot(xn, wu_ref[...])
    h = (g * _sigmoid(g) * u).astype(BF16)
    f = _dot(h, wd_ref[...])
    o_ref[...] = x + 0.5 * _rms(f, post_ref[...])


def _outproj_ffn(sb2d, y2d, bonus, gate, gn_g, gn_b, x2d, w_out, mix_g, pre_g, post_g,
                 wg, wu, wd, tm):
    t, d = x2d.shape
    dff = wg.shape[1]
    const = lambda i: (0, 0)
    row = lambda i: (i, 0)
    once = dict(pipeline_mode=pl.Buffered(1))
    return pl.pallas_call(
        _outproj_ffn_body,
        grid=(t // tm,),
        in_specs=[
            pl.BlockSpec((tm, SB_WIDTH), row),
            pl.BlockSpec((tm, RWKV_WIDTH), row),
            pl.BlockSpec((tm, RWKV_WIDTH), row),
            pl.BlockSpec((tm, RWKV_WIDTH), row),
            pl.BlockSpec((1, RWKV_WIDTH), const),
            pl.BlockSpec((1, RWKV_WIDTH), const),
            pl.BlockSpec((tm, d), row),
            pl.BlockSpec((SB_WIDTH, d), lambda i: (0, 0), **once),
            pl.BlockSpec((RWKV_WIDTH, d), lambda i: (1, 0), **once),
            pl.BlockSpec((1, d), const),
            pl.BlockSpec((1, d), const),
            pl.BlockSpec((1, d), const),
            pl.BlockSpec((d, dff), const, **once),
            pl.BlockSpec((d, dff), const, **once),
            pl.BlockSpec((dff, d), const, **once),
        ],
        out_specs=pl.BlockSpec((tm, d), row),
        out_shape=jax.ShapeDtypeStruct((t, d), F32),
        compiler_params=pltpu.CompilerParams(
            dimension_semantics=("arbitrary",), vmem_limit_bytes=VMEM_LIMIT),
        name="outproj_ffn",
    )(sb2d, y2d, bonus, gate, gn_g, gn_b, x2d, w_out, w_out, mix_g, pre_g, post_g, wg, wu, wd)


def _pick(n, pref):
    return pref if n % pref == 0 else n


def kernel(x, ffn1_pre_g, ffn1_post_g, ffn1_w_gate, ffn1_w_up, ffn1_w_down, mix_pre_g, mix_post_g, w_in, shift_mu, sb_out_g, decay_w0, decay_w2, iclr_a0, iclr_a2, gate_w2, k_k, k_a, r_k, gn_g, gn_b, w_out, ffn2_pre_g, ffn2_post_g, ffn2_w_gate, ffn2_w_up, ffn2_w_down):
    b, s, d = x.shape
    depth = ffn1_pre_g.shape[0]
    t = b * s
    tm = _pick(s, 512)
    blk = _pick(s, 256)
    x2d = x.reshape(t, d)
    for l in range(depth):
        x2d = _ffn(x2d, ffn1_pre_g[l][None], ffn1_post_g[l][None], ffn1_w_gate[l].astype(BF16),
                   ffn1_w_up[l].astype(BF16), ffn1_w_down[l].astype(BF16), tm)
        zpad = jnp.zeros((AAA_LORA, RWKV_WIDTH), F32)
        dw2 = jnp.concatenate([decay_w2[l], zpad], axis=0).astype(BF16)
        aw2 = jnp.concatenate([zpad, iclr_a2[l]], axis=0).astype(BF16)
        q, k, vt, rt, at, bt, kt, v, bh, kh, wc, bonus, gate = _inproj(
            x2d, mix_pre_g[l][None], w_in[l].astype(BF16), shift_mu[l][None], decay_w0[l][None],
            dw2, iclr_a0[l][None], aw2, gate_w2[l].astype(BF16), k_k[l][None], k_a[l][None],
            r_k[l].reshape(1, -1), tm, s)
        sb, y = _mixer(q, k, vt, sb_out_g[l][None], rt, at, bt, kt, v, bh, kh, wc, blk)
        x2d = _outproj_ffn(sb, y.reshape(t, -1), bonus, gate, gn_g[l][None], gn_b[l][None], x2d,
                           w_out[l].astype(BF16), mix_post_g[l][None],
                           ffn2_pre_g[l][None], ffn2_post_g[l][None], ffn2_w_gate[l].astype(BF16),
                           ffn2_w_up[l].astype(BF16), ffn2_w_down[l].astype(BF16), tm)
    return x2d.reshape(b, s, d)
```

```python
import functools

import jax
import jax.numpy as jnp
from jax import lax
from jax.experimental import pallas as pl
from jax.experimental.pallas import tpu as pltpu

F32 = jnp.float32
BF16 = jnp.bfloat16

HEAD_DIM = 64
SB_HEADS = 8
RWKV_HEADS = 8
SB_WIDTH = SB_HEADS * HEAD_DIM
RWKV_WIDTH = RWKV_HEADS * HEAD_DIM
DECAY_LORA = 64
AAA_LORA = 64
GATE_LORA = 128
NORM_EPS = 1e-6
GN_EPS = HEAD_DIM * 1e-5

LANES = 128
PAIR = LANES
GROUP = 2 * LANES
LOG2E = 1.4426950408889634
LOG2E_BF16 = (1.4453125, -0.00262451171875, 7.033348083496094e-06)
CHUNK = 64
VMEM_LIMIT = 56 * 1024 * 1024

_NT = (((1,), (1,)), ((), ()))
_TN = (((0,), (0,)), ((), ()))


def _dot(a, b, dims=None):
    if dims is None:
        dims = (((a.ndim - 1,), (0,)), ((), ()))
    return lax.dot_general(a, b, dims, preferred_element_type=F32)


def _rms(x, g):
    return x * lax.rsqrt(jnp.mean(x * x, axis=-1, keepdims=True) + NORM_EPS) * g


def _softplus(x):
    return jnp.maximum(x, 0.0) + jnp.log(1.0 + jnp.exp(-jnp.abs(x)))


def _sigmoid(x):
    return 1.0 / (1.0 + jnp.exp(-x))


def _split(x):
    hi = x.astype(BF16)
    return hi, (x - hi.astype(F32)).astype(BF16)


def _head_sums(x):
    r = lax.broadcasted_iota(jnp.int32, (GROUP, GROUP), 0) // HEAD_DIM
    c = lax.broadcasted_iota(jnp.int32, (GROUP, GROUP), 1) // HEAD_DIM
    ones = (r == c).astype(BF16)
    parts = [_dot(x[:, gi * GROUP:(gi + 1) * GROUP].astype(BF16), ones)
             for gi in range(x.shape[1] // GROUP)]
    return jnp.concatenate(parts, axis=1)


def _ffn_body(x_ref, pre_ref, post_ref, wg_ref, wu_ref, wd_ref, o_ref):
    x = x_ref[...]
    xn = _rms(x, pre_ref[...]).astype(BF16)
    g = _dot(xn, wg_ref[...])
    u = _dot(xn, wu_ref[...])
    h = (g * _sigmoid(g) * u).astype(BF16)
    f = _dot(h, wd_ref[...])
    o_ref[...] = x + 0.5 * _rms(f, post_ref[...])


def _ffn(x2d, pre_g, post_g, wg, wu, wd, tm):
    t, d = x2d.shape
    dff = wg.shape[1]
    const = lambda i: (0, 0)
    return pl.pallas_call(
        _ffn_body,
        grid=(t // tm,),
        in_specs=[
            pl.BlockSpec((tm, d), lambda i: (i, 0)),
            pl.BlockSpec((1, d), const),
            pl.BlockSpec((1, d), const),
            pl.BlockSpec((d, dff), const, pipeline_mode=pl.Buffered(1)),
            pl.BlockSpec((d, dff), const, pipeline_mode=pl.Buffered(1)),
            pl.BlockSpec((dff, d), const, pipeline_mode=pl.Buffered(1)),
        ],
        out_specs=pl.BlockSpec((tm, d), lambda i: (i, 0)),
        out_shape=jax.ShapeDtypeStruct((t, d), F32),
        compiler_params=pltpu.CompilerParams(
            dimension_semantics=("arbitrary",), vmem_limit_bytes=VMEM_LIMIT),
        name="ffn",
    )(x2d, pre_g, post_g, wg, wu, wd)


def _inproj_body(x_ref, g_ref, w_ref, mu_ref, w0_ref, dw2_ref, a0_ref, aw2_ref, gw2_ref,
                 kk_ref, ka_ref, rk_ref,
                 q_ref, k_ref, vt_ref, rt_ref, at_ref, bt_ref, kt_ref, v_ref, bh_ref, kh_ref,
                 wc_ref, bonus_ref, gate_ref, last_ref, *, per_seq):
    i = pl.program_id(0)
    w = RWKV_WIDTH
    nsb = 3 * SB_WIDTH
    tm = x_ref.shape[0]

    @pl.when(i % per_seq == 0)
    def _():
        last_ref[...] = jnp.zeros_like(last_ref)

    xn = _rms(x_ref[...], g_ref[...]).astype(BF16)
    p = _dot(xn, w_ref[:, nsb:])
    psb = _dot(xn, w_ref[:, :nsb])
    q_ref[...] = (psb[:, :SB_WIDTH] * (HEAD_DIM ** -0.5 * LOG2E)).astype(BF16)
    k_ref[...] = psb[:, SB_WIDTH:2 * SB_WIDTH].astype(BF16)
    vt_ref[0] = psb[:, 2 * SB_WIDTH:].T.astype(BF16)

    rows = lax.broadcasted_iota(jnp.int32, p.shape, 0)
    prev = jnp.where(rows == 0, last_ref[...], pltpu.roll(p, 1, 0))
    last_ref[...] = p[tm - 1:tm, :]
    xs = p + (prev - p) * mu_ref[...]
    r = xs[:, 0:w]
    kr = xs[:, w:2 * w]
    vr = xs[:, 2 * w:3 * w]
    lora_in = xs[:, 3 * w:3 * w + DECAY_LORA + AAA_LORA]
    gd = xs[:, 3 * w + DECAY_LORA + AAA_LORA:]
    lora_w = _dot(jnp.tanh(lora_in).astype(BF16), dw2_ref[...])
    lora_a = _dot(lora_in.astype(BF16), aw2_ref[...])
    gate_ref[...] = _dot(_sigmoid(gd).astype(BF16), gw2_ref[...]).astype(BF16)
    w_log = -_softplus(-(w0_ref[...] + lora_w)) - 0.5
    lw2 = jnp.exp(w_log) * (-LOG2E)
    a = _sigmoid(a0_ref[...] + lora_a)
    kk = kr * kk_ref[...]
    k2 = kr * (1.0 + (a - 1.0) * ka_ref[...])
    bonus_ref[...] = (_head_sums(r * k2 * rk_ref[...]) * vr).astype(BF16)
    kkn = kk * lax.rsqrt(jnp.maximum(_head_sums(kk * kk), 1e-24))
    a_vec = -kkn
    b_vec = kkn * a
    v_ref[...] = vr.astype(BF16)

    sub = 4 * CHUNK
    tr = lax.broadcasted_iota(jnp.int32, (sub, sub), 0)
    tc = lax.broadcasted_iota(jnp.int32, (sub, sub), 1)
    prefix = (((tr // CHUNK) == (tc // CHUNK)) & (tc <= tr)).astype(BF16)
    hi, lo = _split(lw2)
    cums, tots, ends = [], [], []
    for s0 in range(0, tm, sub):
        cum = _dot(prefix, hi[s0:s0 + sub]) + _dot(prefix, lo[s0:s0 + sub])
        cums.append(cum)
        for c in range(sub // CHUNK):
            end = cum[c * CHUNK + CHUNK - 1:(c + 1) * CHUNK, :]
            ends.append(jnp.broadcast_to(end, (8, w)))
            tots.append(jnp.broadcast_to(end, (CHUNK, w)))
    cum = jnp.concatenate(cums, axis=0)
    tot = jnp.concatenate(tots, axis=0)
    dec_out = jnp.exp2(-cum)
    dec_rest = jnp.exp2(tot - cum)
    rt_ref[...] = (r * jnp.exp2(cum)).astype(BF16)
    at_ref[...] = (a_vec * jnp.exp2(cum - lw2)).astype(BF16)
    bt_ref[...] = (b_vec * dec_out).astype(BF16)
    kt_ref[...] = (k2 * dec_out).astype(BF16)
    bh_ref[...] = (b_vec * dec_rest).astype(BF16)
    kh_ref[...] = (k2 * dec_rest).astype(BF16)
    wc_ref[...] = jnp.exp2(jnp.concatenate(ends, axis=0))


def _inproj(x2d, g, w_in, mu, w0, dw2, a0, aw2, gw2, k_k, k_a, r_k, tm, seq):
    t, d = x2d.shape
    n = w_in.shape[1]
    w = RWKV_WIDTH
    per_seq = seq // tm
    const = lambda i: (0, 0)
    row = lambda i: (i, 0)
    vec = lambda m: pl.BlockSpec((1, m), const)
    tile = pl.BlockSpec((tm, w), row)
    wide = jax.ShapeDtypeStruct((t, w), BF16)
    return pl.pallas_call(
        functools.partial(_inproj_body, per_seq=per_seq),
        grid=(t // tm,),
        in_specs=[
            pl.BlockSpec((tm, d), row),
            vec(d),
            pl.BlockSpec((d, n), const, pipeline_mode=pl.Buffered(1)),
            vec(n - 3 * SB_WIDTH), vec(w),
            pl.BlockSpec((DECAY_LORA + AAA_LORA, w), const),
            vec(w),
            pl.BlockSpec((DECAY_LORA + AAA_LORA, w), const),
            pl.BlockSpec((GATE_LORA, w), const),
            vec(w), vec(w), vec(w),
        ],
        out_specs=[
            pl.BlockSpec((tm, SB_WIDTH), row),
            pl.BlockSpec((tm, SB_WIDTH), row),
            pl.BlockSpec((1, SB_WIDTH, tm), lambda i: (i // per_seq, 0, i % per_seq)),
            tile, tile, tile, tile, tile, tile, tile,
            pl.BlockSpec((tm // 8, w), row),
            tile, tile,
        ],
        out_shape=[
            jax.ShapeDtypeStruct((t, SB_WIDTH), BF16),
            jax.ShapeDtypeStruct((t, SB_WIDTH), BF16),
            jax.ShapeDtypeStruct((t // seq, SB_WIDTH, seq), BF16),
            wide, wide, wide, wide, wide, wide, wide,
            jax.ShapeDtypeStruct((t // 8, w), F32),
            wide, wide,
        ],
        scratch_shapes=[pltpu.VMEM((1, n - 3 * SB_WIDTH), F32)],
        compiler_params=pltpu.CompilerParams(
            dimension_semantics=("arbitrary",), vmem_limit_bytes=VMEM_LIMIT),
        name="inproj",
    )(x2d, g, w_in, mu, w0, dw2, a0, aw2, gw2, k_k, k_a, r_k)


def _mixer_body(q_ref, k_ref, vt_ref, g_ref, rt_ref, at_ref, bt_ref, kt_ref, v_ref, bh_ref,
                kh_ref, wc_ref, o_ref, y_ref, acc_ref, qh_ref, state_ref, *stage_refs, blk):
    qi = pl.program_id(1)
    nh = SB_WIDTH // HEAD_DIM
    per_group = GROUP // HEAD_DIM
    halves = tuple((h,) for h in range(nh))
    names = ("zraw", "craw", "z", "sp", "tot")
    bufs = tuple(dict(zip(names, stage_refs[len(names) * h:len(names) * (h + 1)]))
                 for h in range(nh))

    def lanes_of(h):
        g = h // per_group
        return slice(g * GROUP, (g + 1) * GROUP)

    lane = lax.broadcasted_iota(jnp.int32, (blk, GROUP), 1)
    for h in range(nh):
        qg = q_ref[:, lanes_of(h)]
        qh_ref[h] = jnp.where(lane // HEAD_DIM == h % per_group, qg, jnp.zeros_like(qg))
    key = lax.broadcasted_iota(jnp.int32, (blk, blk), 0)
    qry = lax.broadcasted_iota(jnp.int32, (blk, blk), 1)
    neg_after = jnp.where(qry > key, -1.0, 0.0).astype(BF16)
    causal = key < qry

    def rows_of(kb):
        return pl.ds(pl.multiple_of(kb * blk, blk), blk)

    def part(ew=None, mx=None):
        if mx is not None:
            mhalf, kb_s, do_sums = mx
            mb = bufs[mhalf]
            if do_sums:
                sums = [_dot(neg_after, mb["sp"][i]) for i in range(len(halves[mhalf]))]
            if kb_s is not None:
                raw = [_dot(k_ref[rows_of(kb_s), lanes_of(h)], qh_ref[h], _NT)
                       for h in halves[mhalf]]
        new = None
        if ew is not None:
            ehalf, kb_w, do_score, diagonal, carries = ew
            eb = bufs[ehalf]
            new = list(carries)
            outs = []
            if kb_w is not None:
                for i, h in enumerate(halves[ehalf]):
                    new[h] = eb["tot"][i] + carries[h]
                    w = jnp.exp2(eb["z"][i] + eb["craw"][i])
                    vt = vt_ref[0, h * HEAD_DIM:(h + 1) * HEAD_DIM, rows_of(kb_w)]
                    outs.append(_dot(vt, w) * jnp.exp2(carries[h]))
            if do_score:
                for i in range(len(halves[ehalf])):
                    z = eb["zraw"][i]
                    if diagonal:
                        z = jnp.where(causal, z, -1e30)
                    m = jnp.maximum(z, 0.0)
                    n = jnp.minimum(z, 0.0)
                    ln = jnp.log(1.0 + jnp.exp2(n - m))
                    soft = ln * LOG2E_BF16[0] + (ln * LOG2E_BF16[1] + ln * LOG2E_BF16[2])
                    eb["z"][i] = n - soft
                    eb["sp"][i] = m + soft
        if mx is not None:
            for i in range(len(halves[mhalf])):
                if do_sums:
                    mb["tot"][i] = sums[i][0:1, :] - mb["sp"][i][0:1, :].astype(F32)
                    mb["craw"][i] = sums[i].astype(BF16)
                if kb_s is not None:
                    mb["zraw"][i] = raw[i].astype(BF16)
        if ew is not None and kb_w is not None:
            for i, h in enumerate(halves[ehalf]):
                acc_ref[h] += outs[i]
        return None if new is None else tuple(new)

    acc_ref[...] = jnp.zeros_like(acc_ref)
    carries = (jnp.zeros((1, blk), F32),) * nh
    lead = 2

    def one_step(kb_w, kb_s, diagonal, carries, last=False, between=None):
        for h in range(nh):
            if between is not None:
                between()
            ahead = h + lead
            if ahead < nh:
                mx = (ahead, kb_s, kb_w is not None)
            elif last:
                mx = None
            else:
                mx = (ahead - nh, jnp.maximum(kb_s - 1, 0), True)
            carries = part(ew=(h, kb_w, kb_s is not None, diagonal, carries), mx=mx)
        return carries

    @pl.when(qi == 0)
    def _():
        state_ref[...] = jnp.zeros_like(state_ref)

    rwkv = _rwkv_stages(rt_ref, at_ref, bt_ref, kt_ref, v_ref, bh_ref, kh_ref, wc_ref, y_ref,
                        state_ref)

    def rwkv_advance(n):
        for _ in range(n):
            next(rwkv, None)

    for h in range(lead):
        part(mx=(h, qi, False))
        rwkv_advance(1)
    carries = one_step(None, qi, True, carries, between=lambda: rwkv_advance(1))
    odd = qi % 2
    carries = lax.cond(odd == 1, lambda cs: one_step(qi, qi - 1, False, cs), lambda cs: cs,
                       carries)

    def two_steps(j, cs):
        kb_w = qi - odd - 2 * j
        cs = one_step(kb_w, kb_w - 1, False, cs)
        return one_step(kb_w - 1, kb_w - 2, False, cs)

    carries = lax.fori_loop(0, qi // 2, two_steps, carries)
    one_step(0, None, False, carries, last=True, between=lambda: rwkv_advance(2))
    for _ in rwkv:
        pass

    parts = []
    for h in range(nh):
        a = acc_ref[h]
        ms = jnp.mean(a * a, axis=0, keepdims=True)
        parts.append(a * lax.rsqrt(ms + NORM_EPS))
    out = jnp.concatenate(parts, axis=0).T
    o_ref[...] = (out * g_ref[...]).astype(BF16)


def _mixer(q, k, vt, sb_g, rt, at, bt, kt, v, bh, kh, wc, blk):
    t = q.shape[0]
    b, _, s = vt.shape
    w = RWKV_WIDTH
    nq = s // blk
    nh = SB_WIDTH // HEAD_DIM
    span = s // nq
    assert span % CHUNK == 0, (s, blk)
    seq = lambda x: x.reshape(b, s, w)
    tile = pltpu.VMEM((1, blk, blk), BF16)
    row = pltpu.VMEM((1, 1, blk), F32)
    rw_tile = pl.BlockSpec((1, span, w), lambda bi, qi: (bi, qi, 0))
    return pl.pallas_call(
        functools.partial(_mixer_body, blk=blk),
        grid=(b, nq),
        in_specs=[
            pl.BlockSpec((blk, SB_WIDTH), lambda bi, qi: (bi * nq + qi, 0)),
            pl.BlockSpec((s, SB_WIDTH), lambda bi, qi: (bi, 0)),
            pl.BlockSpec((1, SB_WIDTH, s), lambda bi, qi: (bi, 0, 0)),
            pl.BlockSpec((1, SB_WIDTH), lambda bi, qi: (0, 0)),
        ] + [rw_tile] * 7 + [
            pl.BlockSpec((1, span // 8, w), lambda bi, qi: (bi, qi, 0)),
        ],
        out_specs=[
            pl.BlockSpec((blk, SB_WIDTH), lambda bi, qi: (bi * nq + qi, 0)),
            rw_tile,
        ],
        out_shape=[
            jax.ShapeDtypeStruct((t, SB_WIDTH), BF16),
            jax.ShapeDtypeStruct((b, s, w), BF16),
        ],
        scratch_shapes=[
            pltpu.VMEM((nh, HEAD_DIM, blk), F32),
            pltpu.VMEM((nh, blk, GROUP), BF16),
            pltpu.VMEM((w // PAIR, PAIR, PAIR), F32),
        ] + [tile, tile, tile, tile, row] * nh,
        compiler_params=pltpu.CompilerParams(
            dimension_semantics=("arbitrary", "arbitrary"),
            vmem_limit_bytes=VMEM_LIMIT),
        name="mixer",
    )(q, k, vt, sb_g, seq(rt), seq(at), seq(bt), seq(kt), seq(v), seq(bh), seq(kh),
      wc.reshape(b, s // 8, w))


def _rwkv_stages(rt_ref, at_ref, bt_ref, kt_ref, v_ref, bh_ref, kh_ref, wc_ref, y_ref, state_ref):
    w = RWKV_WIDTH
    npair = w // PAIR
    nchunk = rt_ref.shape[1] // CHUNK
    njob = nchunk * npair
    n2 = 2 * CHUNK
    lane = lax.broadcasted_iota(jnp.int32, (CHUNK, PAIR), 1)
    first = lane < HEAD_DIM
    br = lax.broadcasted_iota(jnp.int32, (n2, n2), 0)
    bc = lax.broadcasted_iota(jnp.int32, (n2, n2), 1)
    same_head = (br // CHUNK) == (bc // CHUNK)
    strict = same_head & (bc < br)
    incl = same_head & (bc <= br)
    eye = br == bc

    def stack_masked(x):
        return jnp.concatenate([jnp.where(first, x, 0.0), jnp.where(first, 0.0, x)], axis=0)

    def stack(x):
        return jnp.concatenate([x, x], axis=0)

    sls = [(0, slice((jb // npair) * CHUNK, (jb // npair + 1) * CHUNK),
            slice((jb % npair) * PAIR, (jb % npair + 1) * PAIR)) for jb in range(njob)]
    each = lambda f: [f(jb) for jb in range(njob)]
    a_sm = each(lambda pi: stack_masked(at_ref[sls[pi]]))
    r_sm = each(lambda pi: stack_masked(rt_ref[sls[pi]]))
    v_sm = each(lambda pi: stack_masked(v_ref[sls[pi]]))
    bh_sm = each(lambda pi: stack_masked(bh_ref[sls[pi]]))
    kh_sm = each(lambda pi: stack_masked(kh_ref[sls[pi]]))
    wc = each(lambda pi: wc_ref[0, (pi // npair) * 8:(pi // npair) * 8 + 1, sls[pi][2]])
    mm = each(lambda pi: _dot(
        jnp.concatenate([a_sm[pi], r_sm[pi]], axis=0),
        jnp.concatenate([stack(bt_ref[sls[pi]]), stack(kt_ref[sls[pi]])], axis=0),
        _NT))
    yield
    m_ab = each(lambda pi: jnp.where(strict, mm[pi][:n2, :n2], 0.0))
    m_ak = each(lambda pi: jnp.where(strict, mm[pi][:n2, n2:], 0.0).astype(BF16))
    m_rbk = each(lambda pi: jnp.concatenate(
        [jnp.where(incl, mm[pi][n2:, :n2], 0.0), jnp.where(incl, mm[pi][n2:, n2:], 0.0)],
        axis=1).astype(BF16))
    rhs = each(lambda pi: jnp.concatenate(
        [a_sm[pi], _dot(m_ak[pi], v_sm[pi]).astype(BF16)], axis=1))
    inv = each(lambda pi: jnp.where(eye, 1.0, m_ab[pi]))
    lp16 = each(lambda pi: m_ab[pi].astype(BF16))
    lp16 = each(lambda pi: _dot(lp16[pi], lp16[pi]).astype(BF16))
    yield
    steps = CHUNK.bit_length() - 1
    for i in range(1, steps):
        if i < steps - 1:
            both = each(lambda pi: _dot(
                lp16[pi], jnp.concatenate([lp16[pi], inv[pi].astype(BF16)], axis=1)))
            lp16 = each(lambda pi: both[pi][:, :n2].astype(BF16))
            inv = each(lambda pi: inv[pi] + both[pi][:, n2:])
        else:
            inv = each(lambda pi: inv[pi] + _dot(lp16[pi], inv[pi].astype(BF16)))
        yield
    pq = each(lambda pi: _dot(inv[pi].astype(BF16), rhs[pi]).astype(BF16))
    yield
    zeros = jnp.zeros((n2, PAIR), BF16)
    gy = each(lambda pi: _dot(m_rbk[pi], jnp.concatenate(
        [pq[pi], jnp.concatenate([zeros, v_sm[pi]], axis=1)], axis=0)))
    phi = each(lambda pi: _dot(pq[pi][:, :PAIR], bh_sm[pi], _TN) + jnp.where(eye, wc[pi], 0.0))
    psi = each(lambda pi: _dot(jnp.concatenate([pq[pi][:, PAIR:], v_sm[pi]], axis=0),
                               jnp.concatenate([bh_sm[pi], kh_sm[pi]], axis=0), _TN))
    yield
    g_pk = each(lambda pi: (r_sm[pi][:CHUNK] + gy[pi][:CHUNK, :PAIR]
                            + r_sm[pi][CHUNK:] + gy[pi][CHUNK:, :PAIR]).astype(BF16))
    y0 = each(lambda pi: gy[pi][:CHUNK, PAIR:] + gy[pi][CHUNK:, PAIR:])
    phi16 = each(lambda pi: phi[pi].astype(BF16))
    state = [state_ref[pi] for pi in range(npair)]
    ys = [None] * njob
    for ci in range(nchunk):
        s16 = [st.astype(BF16) for st in state]
        for pi in range(npair):
            jb = ci * npair + pi
            ys[jb] = _dot(g_pk[jb], s16[pi], _NT) + y0[jb]
        state = [_dot(s16[pi], phi16[ci * npair + pi]) + psi[ci * npair + pi]
                 for pi in range(npair)]
        yield
    for jb in range(njob):
        y_ref[sls[jb]] = ys[jb].astype(BF16)
    for pi in range(npair):
        state_ref[pi] = state[pi]


def _outproj_ffn_body(sb_ref, y_ref, bonus_ref, gate_ref, gng_ref, gnb_ref, x_ref, wa_ref, wb_ref,
                      mg_ref, pre_ref, post_ref, wg_ref, wu_ref, wd_ref, o_ref):
    y = y_ref[...].astype(F32)
    inv = 1.0 / HEAD_DIM
    d = y - _head_sums(y) * inv
    var = _head_sums(d * d) * inv
    yn = d * lax.rsqrt(var + GN_EPS) * gng_ref[...] + gnb_ref[...]
    yg = ((yn + bonus_ref[...].astype(F32)) * gate_ref[...].astype(F32)).astype(BF16)
    m = _dot(sb_ref[...], wa_ref[...]) + _dot(yg, wb_ref[...])
    x = x_ref[...] + _rms(m, mg_ref[...])
    xn = _rms(x, pre_ref[...]).astype(BF16)
    g = _dot(xn, wg_ref[...])
    u = _dot(xn, wu_ref[...])
    h = (g * _sigmoid(g) * u).astype(BF16)
    f = _dot(h, wd_ref[...])
    o_ref[...] = x + 0.5 * _rms(f, post_ref[...])


def _outproj_ffn(sb2d, y2d, bonus, gate, gn_g, gn_b, x2d, w_out, mix_g, pre_g, post_g,
                 wg, wu, wd, tm):
    t, d = x2d.shape
    dff = wg.shape[1]
    const = lambda i: (0, 0)
    row = lambda i: (i, 0)
    once = dict(pipeline_mode=pl.Buffered(1))
    return pl.pallas_call(
        _outproj_ffn_body,
        grid=(t // tm,),
        in_specs=[
            pl.BlockSpec((tm, SB_WIDTH), row),
            pl.BlockSpec((tm, RWKV_WIDTH), row),
            pl.BlockSpec((tm, RWKV_WIDTH), row),
            pl.BlockSpec((tm, RWKV_WIDTH), row),
            pl.BlockSpec((1, RWKV_WIDTH), const),
            pl.BlockSpec((1, RWKV_WIDTH), const),
            pl.BlockSpec((tm, d), row),
            pl.BlockSpec((SB_WIDTH, d), lambda i: (0, 0), **once),
            pl.BlockSpec((RWKV_WIDTH, d), lambda i: (1, 0), **once),
            pl.BlockSpec((1, d), const),
            pl.BlockSpec((1, d), const),
            pl.BlockSpec((1, d), const),
            pl.BlockSpec((d, dff), const, **once),
            pl.BlockSpec((d, dff), const, **once),
            pl.BlockSpec((dff, d), const, **once),
        ],
        out_specs=pl.BlockSpec((tm, d), row),
        out_shape=jax.ShapeDtypeStruct((t, d), F32),
        compiler_params=pltpu.CompilerParams(
            dimension_semantics=("arbitrary",), vmem_limit_bytes=VMEM_LIMIT),
        name="outproj_ffn",
    )(sb2d, y2d, bonus, gate, gn_g, gn_b, x2d, w_out, w_out, mix_g, pre_g, post_g, wg, wu, wd)


def _pick(n, pref):
    return pref if n % pref == 0 else n


def kernel(x, ffn1_pre_g, ffn1_post_g, ffn1_w_gate, ffn1_w_up, ffn1_w_down, mix_pre_g, mix_post_g, w_in, shift_mu, sb_out_g, decay_w0, decay_w2, iclr_a0, iclr_a2, gate_w2, k_k, k_a, r_k, gn_g, gn_b, w_out, ffn2_pre_g, ffn2_post_g, ffn2_w_gate, ffn2_w_up, ffn2_w_down):
    b, s, d = x.shape
    depth = ffn1_pre_g.shape[0]
    t = b * s
    tm = _pick(s, 512)
    blk = _pick(s, 256)
    x2d = x.reshape(t, d)
    for l in range(depth):
        x2d = _ffn(x2d, ffn1_pre_g[l][None], ffn1_post_g[l][None], ffn1_w_gate[l].astype(BF16),
                   ffn1_w_up[l].astype(BF16), ffn1_w_down[l].astype(BF16), tm)
        zpad = jnp.zeros((AAA_LORA, RWKV_WIDTH), F32)
        dw2 = jnp.concatenate([decay_w2[l], zpad], axis=0).astype(BF16)
        aw2 = jnp.concatenate([zpad, iclr_a2[l]], axis=0).astype(BF16)
        q, k, vt, rt, at, bt, kt, v, bh, kh, wc, bonus, gate = _inproj(
            x2d, mix_pre_g[l][None], w_in[l].astype(BF16), shift_mu[l][None], decay_w0[l][None],
            dw2, iclr_a0[l][None], aw2, gate_w2[l].astype(BF16), k_k[l][None], k_a[l][None],
            r_k[l].reshape(1, -1), tm, s)
        sb, y = _mixer(q, k, vt, sb_out_g[l][None], rt, at, bt, kt, v, bh, kh, wc, blk)
        x2d = _outproj_ffn(sb, y.reshape(t, -1), bonus, gate, gn_g[l][None], gn_b[l][None], x2d,
                           w_out[l].astype(BF16), mix_post_g[l][None],
                           ffn2_pre_g[l][None], ffn2_post_g[l][None], ffn2_w_gate[l].astype(BF16),
                           ffn2_w_up[l].astype(BF16), ffn2_w_down[l].astype(BF16), tm)
    return x2d.reshape(b, s, d)
```

```python
import functools

import jax
import jax.numpy as jnp
from jax import lax
from jax.experimental import pallas as pl
from jax.experimental.pallas import tpu as pltpu

F32 = jnp.float32
BF16 = jnp.bfloat16

HEAD_DIM = 64
SB_HEADS = 8
RWKV_HEADS = 8
SB_WIDTH = SB_HEADS * HEAD_DIM
RWKV_WIDTH = RWKV_HEADS * HEAD_DIM
DECAY_LORA = 64
AAA_LORA = 64
GATE_LORA = 128
NORM_EPS = 1e-6
GN_EPS = HEAD_DIM * 1e-5

LANES = 128
PAIR = LANES
GROUP = 2 * LANES
LOG2E = 1.4426950408889634
LOG2E_BF16 = (1.4453125, -0.00262451171875, 7.033348083496094e-06)
CHUNK = 64
VMEM_LIMIT = 56 * 1024 * 1024

_NT = (((1,), (1,)), ((), ()))
_TN = (((0,), (0,)), ((), ()))


def _dot(a, b, dims=None):
    if dims is None:
        dims = (((a.ndim - 1,), (0,)), ((), ()))
    return lax.dot_general(a, b, dims, preferred_element_type=F32)


def _rms(x, g):
    return x * lax.rsqrt(jnp.mean(x * x, axis=-1, keepdims=True) + NORM_EPS) * g


def _softplus(x):
    return jnp.maximum(x, 0.0) + jnp.log(1.0 + jnp.exp(-jnp.abs(x)))


def _sigmoid(x):
    return 1.0 / (1.0 + jnp.exp(-x))


def _split(x):
    hi = x.astype(BF16)
    return hi, (x - hi.astype(F32)).astype(BF16)


def _head_sums(x):
    r = lax.broadcasted_iota(jnp.int32, (GROUP, GROUP), 0) // HEAD_DIM
    c = lax.broadcasted_iota(jnp.int32, (GROUP, GROUP), 1) // HEAD_DIM
    ones = (r == c).astype(BF16)
    parts = [_dot(x[:, gi * GROUP:(gi + 1) * GROUP].astype(BF16), ones)
             for gi in range(x.shape[1] // GROUP)]
    return jnp.concatenate(parts, axis=1)


def _ffn_body(x_ref, pre_ref, post_ref, wg_ref, wu_ref, wd_ref, o_ref, f_ref, xs_ref):
    @pl.when(pl.program_id(0) == 0)
    def _():
        f_ref[...] = jnp.zeros_like(f_ref)
        xs_ref[...] = jnp.zeros_like(xs_ref)

    x = x_ref[...]
    xn = _rms(x, pre_ref[...]).astype(BF16)
    g = _dot(xn, wg_ref[...])
    u = _dot(xn, wu_ref[...])
    done = xs_ref[...] + 0.5 * _rms(f_ref[...], post_ref[...])
    o_ref[...] = done
    bits = pltpu.bitcast(done[:, :LANES], jnp.uint32)
    zero = pltpu.bitcast((bits >> 16) >> 16, F32)
    act = g * _sigmoid(g) * u
    act = jnp.concatenate([act[:, :LANES] + zero, act[:, LANES:]], axis=1)
    f_ref[...] = _dot(act.astype(BF16), wd_ref[...])
    xs_ref[...] = x


def _ffn(x2d, pre_g, post_g, wg, wu, wd, tm):
    t, d = x2d.shape
    dff = wg.shape[1]
    const = lambda i: (0, 0)
    ntile = t // tm
    return pl.pallas_call(
        _ffn_body,
        grid=(ntile + 1,),
        in_specs=[
            pl.BlockSpec((tm, d), lambda i: (jnp.minimum(i, ntile - 1), 0)),
            pl.BlockSpec((1, d), const),
            pl.BlockSpec((1, d), const),
            pl.BlockSpec((d, dff), const, pipeline_mode=pl.Buffered(1)),
            pl.BlockSpec((d, dff), const, pipeline_mode=pl.Buffered(1)),
            pl.BlockSpec((dff, d), const, pipeline_mode=pl.Buffered(1)),
        ],
        out_specs=pl.BlockSpec((tm, d), lambda i: (jnp.maximum(i - 1, 0), 0)),
        out_shape=jax.ShapeDtypeStruct((t, d), F32),
        scratch_shapes=[
            pltpu.VMEM((tm, d), F32),
            pltpu.VMEM((tm, d), F32),
        ],
        compiler_params=pltpu.CompilerParams(
            dimension_semantics=("arbitrary",), vmem_limit_bytes=VMEM_LIMIT),
        name="ffn",
    )(x2d, pre_g, post_g, wg, wu, wd)


def _inproj_body(x_ref, g_ref, w_ref, mu_ref, w0_ref, dw2_ref, a0_ref, aw2_ref, gw2_ref,
                 kk_ref, ka_ref, rk_ref,
                 q_ref, k_ref, vt_ref, rt_ref, at_ref, bt_ref, kt_ref, v_ref, bh_ref, kh_ref,
                 wc_ref, bonus_ref, gate_ref, last_ref, *, per_seq):
    i = pl.program_id(0)
    w = RWKV_WIDTH
    nsb = 3 * SB_WIDTH
    tm = x_ref.shape[0]

    @pl.when(i % per_seq == 0)
    def _():
        last_ref[...] = jnp.zeros_like(last_ref)

    xn = _rms(x_ref[...], g_ref[...]).astype(BF16)
    p = _dot(xn, w_ref[:, nsb:])
    psb = _dot(xn, w_ref[:, :nsb])
    q_ref[...] = (psb[:, :SB_WIDTH] * (HEAD_DIM ** -0.5 * LOG2E)).astype(BF16)
    k_ref[...] = psb[:, SB_WIDTH:2 * SB_WIDTH].astype(BF16)
    vt_ref[0] = psb[:, 2 * SB_WIDTH:].T.astype(BF16)

    rows = lax.broadcasted_iota(jnp.int32, p.shape, 0)
    prev = jnp.where(rows == 0, last_ref[...], pltpu.roll(p, 1, 0))
    last_ref[...] = p[tm - 1:tm, :]
    xs = p + (prev - p) * mu_ref[...]
    r = xs[:, 0:w]
    kr = xs[:, w:2 * w]
    vr = xs[:, 2 * w:3 * w]
    lora_in = xs[:, 3 * w:3 * w + DECAY_LORA + AAA_LORA]
    gd = xs[:, 3 * w + DECAY_LORA + AAA_LORA:]
    lora_w = _dot(jnp.tanh(lora_in).astype(BF16), dw2_ref[...])
    lora_a = _dot(lora_in.astype(BF16), aw2_ref[...])
    gate_ref[...] = _dot(_sigmoid(gd).astype(BF16), gw2_ref[...]).astype(BF16)
    w_log = -_softplus(-(w0_ref[...] + lora_w)) - 0.5
    lw2 = jnp.exp(w_log) * (-LOG2E)
    a = _sigmoid(a0_ref[...] + lora_a)
    kk = kr * kk_ref[...]
    k2 = kr * (1.0 + (a - 1.0) * ka_ref[...])
    bonus_ref[...] = (_head_sums(r * k2 * rk_ref[...]) * vr).astype(BF16)
    kkn = kk * lax.rsqrt(jnp.maximum(_head_sums(kk * kk), 1e-24))
    a_vec = -kkn
    b_vec = kkn * a
    v_ref[...] = vr.astype(BF16)

    sub = 4 * CHUNK
    tr = lax.broadcasted_iota(jnp.int32, (sub, sub), 0)
    tc = lax.broadcasted_iota(jnp.int32, (sub, sub), 1)
    prefix = (((tr // CHUNK) == (tc // CHUNK)) & (tc <= tr)).astype(BF16)
    hi, lo = _split(lw2)
    cums, tots, ends = [], [], []
    for s0 in range(0, tm, sub):
        cum = _dot(prefix, hi[s0:s0 + sub]) + _dot(prefix, lo[s0:s0 + sub])
        cums.append(cum)
        for c in range(sub // CHUNK):
            end = cum[c * CHUNK + CHUNK - 1:(c + 1) * CHUNK, :]
            ends.append(jnp.broadcast_to(end, (8, w)))
            tots.append(jnp.broadcast_to(end, (CHUNK, w)))
    cum = jnp.concatenate(cums, axis=0)
    tot = jnp.concatenate(tots, axis=0)
    dec_out = jnp.exp2(-cum)
    dec_rest = jnp.exp2(tot - cum)
    rt_ref[...] = (r * jnp.exp2(cum)).astype(BF16)
    at_ref[...] = (a_vec * jnp.exp2(cum - lw2)).astype(BF16)
    bt_ref[...] = (b_vec * dec_out).astype(BF16)
    kt_ref[...] = (k2 * dec_out).astype(BF16)
    bh_ref[...] = (b_vec * dec_rest).astype(BF16)
    kh_ref[...] = (k2 * dec_rest).astype(BF16)
    wc_ref[...] = jnp.exp2(jnp.concatenate(ends, axis=0))


def _inproj(x2d, g, w_in, mu, w0, dw2, a0, aw2, gw2, k_k, k_a, r_k, tm, seq):
    t, d = x2d.shape
    n = w_in.shape[1]
    w = RWKV_WIDTH
    per_seq = seq // tm
    const = lambda i: (0, 0)
    row = lambda i: (i, 0)
    vec = lambda m: pl.BlockSpec((1, m), const)
    tile = pl.BlockSpec((tm, w), row)
    wide = jax.ShapeDtypeStruct((t, w), BF16)
    return pl.pallas_call(
        functools.partial(_inproj_body, per_seq=per_seq),
        grid=(t // tm,),
        in_specs=[
            pl.BlockSpec((tm, d), row),
            vec(d),
            pl.BlockSpec((d, n), const, pipeline_mode=pl.Buffered(1)),
            vec(n - 3 * SB_WIDTH), vec(w),
            pl.BlockSpec((DECAY_LORA + AAA_LORA, w), const),
            vec(w),
            pl.BlockSpec((DECAY_LORA + AAA_LORA, w), const),
            pl.BlockSpec((GATE_LORA, w), const),
            vec(w), vec(w), vec(w),
        ],
        out_specs=[
            pl.BlockSpec((tm, SB_WIDTH), row),
            pl.BlockSpec((tm, SB_WIDTH), row),
            pl.BlockSpec((1, SB_WIDTH, tm), lambda i: (i // per_seq, 0, i % per_seq)),
            tile, tile, tile, tile, tile, tile, tile,
            pl.BlockSpec((tm // 8, w), row),
            tile, tile,
        ],
        out_shape=[
            jax.ShapeDtypeStruct((t, SB_WIDTH), BF16),
            jax.ShapeDtypeStruct((t, SB_WIDTH), BF16),
            jax.ShapeDtypeStruct((t // seq, SB_WIDTH, seq), BF16),
            wide, wide, wide, wide, wide, wide, wide,
            jax.ShapeDtypeStruct((t // 8, w), F32),
            wide, wide,
        ],
        scratch_shapes=[pltpu.VMEM((1, n - 3 * SB_WIDTH), F32)],
        compiler_params=pltpu.CompilerParams(
            dimension_semantics=("arbitrary",), vmem_limit_bytes=VMEM_LIMIT),
        name="inproj",
    )(x2d, g, w_in, mu, w0, dw2, a0, aw2, gw2, k_k, k_a, r_k)


def _mixer_body(q_ref, k_ref, vt_ref, g_ref, rt_ref, at_ref, bt_ref, kt_ref, v_ref, bh_ref,
                kh_ref, wc_ref, o_ref, y_ref, acc_ref, qh_ref, state_ref, *stage_refs, blk):
    qi = pl.program_id(1)
    nh = SB_WIDTH // HEAD_DIM
    per_group = GROUP // HEAD_DIM
    halves = tuple((h,) for h in range(nh))
    names = ("zraw", "craw", "z", "sp", "tot")
    bufs = tuple(dict(zip(names, stage_refs[len(names) * h:len(names) * (h + 1)]))
                 for h in range(nh))

    def lanes_of(h):
        g = h // per_group
        return slice(g * GROUP, (g + 1) * GROUP)

    lane = lax.broadcasted_iota(jnp.int32, (blk, GROUP), 1)
    for h in range(nh):
        qg = q_ref[:, lanes_of(h)]
        qh_ref[h] = jnp.where(lane // HEAD_DIM == h % per_group, qg, jnp.zeros_like(qg))
    key = lax.broadcasted_iota(jnp.int32, (blk, blk), 0)
    qry = lax.broadcasted_iota(jnp.int32, (blk, blk), 1)
    neg_after = jnp.where(qry > key, -1.0, 0.0).astype(BF16)
    causal = key < qry

    def rows_of(kb):
        return pl.ds(pl.multiple_of(kb * blk, blk), blk)

    def part(ew=None, mx=None):
        if mx is not None:
            mhalf, kb_s, do_sums = mx
            mb = bufs[mhalf]
            if do_sums:
                sums = [_dot(neg_after, mb["sp"][i]) for i in range(len(halves[mhalf]))]
            if kb_s is not None:
                raw = [_dot(k_ref[rows_of(kb_s), lanes_of(h)], qh_ref[h], _NT)
                       for h in halves[mhalf]]
        new = None
        if ew is not None:
            ehalf, kb_w, do_score, diagonal, carries = ew
            eb = bufs[ehalf]
            new = list(carries)
            outs = []
            if kb_w is not None:
                for i, h in enumerate(halves[ehalf]):
                    new[h] = eb["tot"][i] + carries[h]
                    w = jnp.exp2(eb["z"][i] + eb["craw"][i])
                    vt = vt_ref[0, h * HEAD_DIM:(h + 1) * HEAD_DIM, rows_of(kb_w)]
                    outs.append(_dot(vt, w) * jnp.exp2(carries[h]))
            if do_score:
                for i in range(len(halves[ehalf])):
                    z = eb["zraw"][i]
                    if diagonal:
                        z = jnp.where(causal, z, -1e30)
                    m = jnp.maximum(z, 0.0)
                    n = jnp.minimum(z, 0.0)
                    ln = jnp.log(1.0 + jnp.exp2(n - m))
                    soft = ln * LOG2E_BF16[0] + (ln * LOG2E_BF16[1] + ln * LOG2E_BF16[2])
                    eb["z"][i] = n - soft
                    eb["sp"][i] = m + soft
        if mx is not None:
            for i in range(len(halves[mhalf])):
                if do_sums:
                    mb["tot"][i] = sums[i][0:1, :] - mb["sp"][i][0:1, :].astype(F32)
                    mb["craw"][i] = sums[i].astype(BF16)
                if kb_s is not None:
                    mb["zraw"][i] = raw[i].astype(BF16)
        if ew is not None and kb_w is not None:
            for i, h in enumerate(halves[ehalf]):
                acc_ref[h] += outs[i]
        return None if new is None else tuple(new)

    acc_ref[...] = jnp.zeros_like(acc_ref)
    carries = (jnp.zeros((1, blk), F32),) * nh
    lead = 2

    def one_step(kb_w, kb_s, diagonal, carries, last=False, between=None):
        for h in range(nh):
            if between is not None:
                between()
            ahead = h + lead
            if ahead < nh:
                mx = (ahead, kb_s, kb_w is not None)
            elif last:
                mx = None
            else:
                mx = (ahead - nh, jnp.maximum(kb_s - 1, 0), True)
            carries = part(ew=(h, kb_w, kb_s is not None, diagonal, carries), mx=mx)
        return carries

    @pl.when(qi == 0)
    def _():
        state_ref[...] = jnp.zeros_like(state_ref)

    rwkv = _rwkv_stages(rt_ref, at_ref, bt_ref, kt_ref, v_ref, bh_ref, kh_ref, wc_ref, y_ref,
                        state_ref)

    def rwkv_advance(n):
        for _ in range(n):
            next(rwkv, None)

    for h in range(lead):
        part(mx=(h, qi, False))
        rwkv_advance(1)
    carries = one_step(None, qi, True, carries, between=lambda: rwkv_advance(1))
    odd = qi % 2
    carries = lax.cond(odd == 1, lambda cs: one_step(qi, qi - 1, False, cs), lambda cs: cs,
                       carries)

    def two_steps(j, cs):
        kb_w = qi - odd - 2 * j
        cs = one_step(kb_w, kb_w - 1, False, cs)
        return one_step(kb_w - 1, kb_w - 2, False, cs)

    carries = lax.fori_loop(0, qi // 2, two_steps, carries)
    one_step(0, None, False, carries, last=True, between=lambda: rwkv_advance(2))
    for _ in rwkv:
        pass

    parts = []
    for h in range(nh):
        a = acc_ref[h]
        ms = jnp.mean(a * a, axis=0, keepdims=True)
        parts.append(a * lax.rsqrt(ms + NORM_EPS))
    out = jnp.concatenate(parts, axis=0).T
    o_ref[...] = (out * g_ref[...]).astype(BF16)


def _mixer(q, k, vt, sb_g, rt, at, bt, kt, v, bh, kh, wc, blk):
    t = q.shape[0]
    b, _, s = vt.shape
    w = RWKV_WIDTH
    nq = s // blk
    nh = SB_WIDTH // HEAD_DIM
    span = s // nq
    assert span % CHUNK == 0, (s, blk)
    seq = lambda x: x.reshape(b, s, w)
    tile = pltpu.VMEM((1, blk, blk), BF16)
    row = pltpu.VMEM((1, 1, blk), F32)
    rw_tile = pl.BlockSpec((1, span, w), lambda bi, qi: (bi, qi, 0))
    return pl.pallas_call(
        functools.partial(_mixer_body, blk=blk),
        grid=(b, nq),
        in_specs=[
            pl.BlockSpec((blk, SB_WIDTH), lambda bi, qi: (bi * nq + qi, 0)),
            pl.BlockSpec((s, SB_WIDTH), lambda bi, qi: (bi, 0)),
            pl.BlockSpec((1, SB_WIDTH, s), lambda bi, qi: (bi, 0, 0)),
            pl.BlockSpec((1, SB_WIDTH), lambda bi, qi: (0, 0)),
        ] + [rw_tile] * 7 + [
            pl.BlockSpec((1, span // 8, w), lambda bi, qi: (bi, qi, 0)),
        ],
        out_specs=[
            pl.BlockSpec((blk, SB_WIDTH), lambda bi, qi: (bi * nq + qi, 0)),
            rw_tile,
        ],
        out_shape=[
            jax.ShapeDtypeStruct((t, SB_WIDTH), BF16),
            jax.ShapeDtypeStruct((b, s, w), BF16),
        ],
        scratch_shapes=[
            pltpu.VMEM((nh, HEAD_DIM, blk), F32),
            pltpu.VMEM((nh, blk, GROUP), BF16),
            pltpu.VMEM((w // PAIR, PAIR, PAIR), F32),
        ] + [tile, tile, tile, tile, row] * nh,
        compiler_params=pltpu.CompilerParams(
            dimension_semantics=("arbitrary", "arbitrary"),
            vmem_limit_bytes=VMEM_LIMIT),
        name="mixer",
    )(q, k, vt, sb_g, seq(rt), seq(at), seq(bt), seq(kt), seq(v), seq(bh), seq(kh),
      wc.reshape(b, s // 8, w))


def _rwkv_stages(rt_ref, at_ref, bt_ref, kt_ref, v_ref, bh_ref, kh_ref, wc_ref, y_ref, state_ref):
    w = RWKV_WIDTH
    npair = w // PAIR
    nchunk = rt_ref.shape[1] // CHUNK
    njob = nchunk * npair
    n2 = 2 * CHUNK
    lane = lax.broadcasted_iota(jnp.int32, (CHUNK, PAIR), 1)
    first = lane < HEAD_DIM
    br = lax.broadcasted_iota(jnp.int32, (n2, n2), 0)
    bc = lax.broadcasted_iota(jnp.int32, (n2, n2), 1)
    same_head = (br // CHUNK) == (bc // CHUNK)
    strict = same_head & (bc < br)
    incl = same_head & (bc <= br)
    eye = br == bc

    def stack_masked(x):
        return jnp.concatenate([jnp.where(first, x, 0.0), jnp.where(first, 0.0, x)], axis=0)

    def stack(x):
        return jnp.concatenate([x, x], axis=0)

    sls = [(0, slice((jb // npair) * CHUNK, (jb // npair + 1) * CHUNK),
            slice((jb % npair) * PAIR, (jb % npair + 1) * PAIR)) for jb in range(njob)]
    each = lambda f: [f(jb) for jb in range(njob)]
    a_sm = each(lambda pi: stack_masked(at_ref[sls[pi]]))
    r_sm = each(lambda pi: stack_masked(rt_ref[sls[pi]]))
    v_sm = each(lambda pi: stack_masked(v_ref[sls[pi]]))
    bh_sm = each(lambda pi: stack_masked(bh_ref[sls[pi]]))
    kh_sm = each(lambda pi: stack_masked(kh_ref[sls[pi]]))
    wc = each(lambda pi: wc_ref[0, (pi // npair) * 8:(pi // npair) * 8 + 1, sls[pi][2]])
    mm = each(lambda pi: _dot(
        jnp.concatenate([a_sm[pi], r_sm[pi]], axis=0),
        jnp.concatenate([stack(bt_ref[sls[pi]]), stack(kt_ref[sls[pi]])], axis=0),
        _NT))
    yield
    m_ab = each(lambda pi: jnp.where(strict, mm[pi][:n2, :n2], 0.0))
    m_ak = each(lambda pi: jnp.where(strict, mm[pi][:n2, n2:], 0.0).astype(BF16))
    m_rbk = each(lambda pi: jnp.concatenate(
        [jnp.where(incl, mm[pi][n2:, :n2], 0.0), jnp.where(incl, mm[pi][n2:, n2:], 0.0)],
        axis=1).astype(BF16))
    rhs = each(lambda pi: jnp.concatenate(
        [a_sm[pi], _dot(m_ak[pi], v_sm[pi]).astype(BF16)], axis=1))
    inv = each(lambda pi: jnp.where(eye, 1.0, m_ab[pi]))
    lp16 = each(lambda pi: m_ab[pi].astype(BF16))
    lp16 = each(lambda pi: _dot(lp16[pi], lp16[pi]).astype(BF16))
    yield
    steps = CHUNK.bit_length() - 1
    for i in range(1, steps):
        if i < steps - 1:
            both = each(lambda pi: _dot(
                lp16[pi], jnp.concatenate([lp16[pi], inv[pi].astype(BF16)], axis=1)))
            lp16 = each(lambda pi: both[pi][:, :n2].astype(BF16))
            inv = each(lambda pi: inv[pi] + both[pi][:, n2:])
        else:
            inv = each(lambda pi: inv[pi] + _dot(lp16[pi], inv[pi].astype(BF16)))
        yield
    pq = each(lambda pi: _dot(inv[pi].astype(BF16), rhs[pi]).astype(BF16))
    yield
    zeros = jnp.zeros((n2, PAIR), BF16)
    gy = each(lambda pi: _dot(m_rbk[pi], jnp.concatenate(
        [pq[pi], jnp.concatenate([zeros, v_sm[pi]], axis=1)], axis=0)))
    phi = each(lambda pi: _dot(pq[pi][:, :PAIR], bh_sm[pi], _TN) + jnp.where(eye, wc[pi], 0.0))
    psi = each(lambda pi: _dot(jnp.concatenate([pq[pi][:, PAIR:], v_sm[pi]], axis=0),
                               jnp.concatenate([bh_sm[pi], kh_sm[pi]], axis=0), _TN))
    yield
    g_pk = each(lambda pi: (r_sm[pi][:CHUNK] + gy[pi][:CHUNK, :PAIR]
                            + r_sm[pi][CHUNK:] + gy[pi][CHUNK:, :PAIR]).astype(BF16))
    y0 = each(lambda pi: gy[pi][:CHUNK, PAIR:] + gy[pi][CHUNK:, PAIR:])
    phi16 = each(lambda pi: phi[pi].astype(BF16))
    state = [state_ref[pi] for pi in range(npair)]
    ys = [None] * njob
    for ci in range(nchunk):
        s16 = [st.astype(BF16) for st in state]
        for pi in range(npair):
            jb = ci * npair + pi
            ys[jb] = _dot(g_pk[jb], s16[pi], _NT) + y0[jb]
        state = [_dot(s16[pi], phi16[ci * npair + pi]) + psi[ci * npair + pi]
                 for pi in range(npair)]
        yield
    for jb in range(njob):
        y_ref[sls[jb]] = ys[jb].astype(BF16)
    for pi in range(npair):
        state_ref[pi] = state[pi]


def _outproj_ffn_body(sb_ref, y_ref, bonus_ref, gate_ref, gng_ref, gnb_ref, x_ref, wa_ref, wb_ref,
                      mg_ref, pre_ref, post_ref, wg_ref, wu_ref, wd_ref, o_ref):
    y = y_ref[...].astype(F32)
    inv = 1.0 / HEAD_DIM
    d = y - _head_sums(y) * inv
    var = _head_sums(d * d) * inv
    yn = d * lax.rsqrt(var + GN_EPS) * gng_ref[...] + gnb_ref[...]
    yg = ((yn + bonus_ref[...].astype(F32)) * gate_ref[...].astype(F32)).astype(BF16)
    m = _dot(sb_ref[...], wa_ref[...]) + _dot(yg, wb_ref[...])
    x = x_ref[...] + _rms(m, mg_ref[...])
    xn = _rms(x, pre_ref[...]).astype(BF16)
    g = _dot(xn, wg_ref[...])
    u = _dot(xn, wu_ref[...])
    h = (g * _sigmoid(g) * u).astype(BF16)
    f = _dot(h, wd_ref[...])
    o_ref[...] = x + 0.5 * _rms(f, post_ref[...])


def _outproj_ffn(sb2d, y2d, bonus, gate, gn_g, gn_b, x2d, w_out, mix_g, pre_g, post_g,
                 wg, wu, wd, tm):
    t, d = x2d.shape
    dff = wg.shape[1]
    const = lambda i: (0, 0)
    row = lambda i: (i, 0)
    once = dict(pipeline_mode=pl.Buffered(1))
    return pl.pallas_call(
        _outproj_ffn_body,
        grid=(t // tm,),
        in_specs=[
            pl.BlockSpec((tm, SB_WIDTH), row),
            pl.BlockSpec((tm, RWKV_WIDTH), row),
            pl.BlockSpec((tm, RWKV_WIDTH), row),
            pl.BlockSpec((tm, RWKV_WIDTH), row),
            pl.BlockSpec((1, RWKV_WIDTH), const),
            pl.BlockSpec((1, RWKV_WIDTH), const),
            pl.BlockSpec((tm, d), row),
            pl.BlockSpec((SB_WIDTH, d), lambda i: (0, 0), **once),
            pl.BlockSpec((RWKV_WIDTH, d), lambda i: (1, 0), **once),
            pl.BlockSpec((1, d), const),
            pl.BlockSpec((1, d), const),
            pl.BlockSpec((1, d), const),
            pl.BlockSpec((d, dff), const, **once),
            pl.BlockSpec((d, dff), const, **once),
            pl.BlockSpec((dff, d), const, **once),
        ],
        out_specs=pl.BlockSpec((tm, d), row),
        out_shape=jax.ShapeDtypeStruct((t, d), F32),
        compiler_params=pltpu.CompilerParams(
            dimension_semantics=("arbitrary",), vmem_limit_bytes=VMEM_LIMIT),
        name="outproj_ffn",
    )(sb2d, y2d, bonus, gate, gn_g, gn_b, x2d, w_out, w_out, mix_g, pre_g, post_g, wg, wu, wd)


def _pick(n, pref):
    return pref if n % pref == 0 else n


def kernel(x, ffn1_pre_g, ffn1_post_g, ffn1_w_gate, ffn1_w_up, ffn1_w_down, mix_pre_g, mix_post_g, w_in, shift_mu, sb_out_g, decay_w0, decay_w2, iclr_a0, iclr_a2, gate_w2, k_k, k_a, r_k, gn_g, gn_b, w_out, ffn2_pre_g, ffn2_post_g, ffn2_w_gate, ffn2_w_up, ffn2_w_down):
    b, s, d = x.shape
    depth = ffn1_pre_g.shape[0]
    t = b * s
    tm = _pick(s, 512)
    blk = _pick(s, 256)
    x2d = x.reshape(t, d)
    for l in range(depth):
        x2d = _ffn(x2d, ffn1_pre_g[l][None], ffn1_post_g[l][None], ffn1_w_gate[l].astype(BF16),
                   ffn1_w_up[l].astype(BF16), ffn1_w_down[l].astype(BF16), tm)
        zpad = jnp.zeros((AAA_LORA, RWKV_WIDTH), F32)
        dw2 = jnp.concatenate([decay_w2[l], zpad], axis=0).astype(BF16)
        aw2 = jnp.concatenate([zpad, iclr_a2[l]], axis=0).astype(BF16)
        q, k, vt, rt, at, bt, kt, v, bh, kh, wc, bonus, gate = _inproj(
            x2d, mix_pre_g[l][None], w_in[l].astype(BF16), shift_mu[l][None], decay_w0[l][None],
            dw2, iclr_a0[l][None], aw2, gate_w2[l].astype(BF16), k_k[l][None], k_a[l][None],
            r_k[l].reshape(1, -1), tm, s)
        sb, y = _mixer(q, k, vt, sb_out_g[l][None], rt, at, bt, kt, v, bh, kh, wc, blk)
        x2d = _outproj_ffn(sb, y.reshape(t, -1), bonus, gate, gn_g[l][None], gn_b[l][None], x2d,
                           w_out[l].astype(BF16), mix_post_g[l][None],
                           ffn2_pre_g[l][None], ffn2_post_g[l][None], ffn2_w_gate[l].astype(BF16),
                           ffn2_w_up[l].astype(BF16), ffn2_w_down[l].astype(BF16), tm)
    return x2d.reshape(b, s, d)
```
